```python
import math
import jax, jax.numpy as jnp
from jax import lax
import numpy as np

D_MODEL = 1024
BATCH = 8
SEQ = 4096
DEPTH = 2
DEC_BATCH = 128
DEC_SEQ = 1
PAST_LEN = 16384
PAGE_SIZE = 128

MLA_HEADS = 8
MLA_NOPE = 64
MLA_ROPE = 32
MLA_V = 64
Q_LORA = 384
KV_LORA = 256
DSA_HEADS = 8
DSA_KV_HEADS = 2
DSA_HEAD_DIM = 64
DSA_GROUP = DSA_HEADS // DSA_KV_HEADS
IDX_HEADS = 8
IDX_DIM = 64
TOPK_MAX = 256
N_BUCKETS = 32
MAX_DISTANCE = 128
D_FF = 2816
N_MOD = 9
Q_BLOCK = 128
ROPE_THETA = 10000.0
EPS = 1e-6
MLA_SCALE = (MLA_NOPE + MLA_ROPE) ** -0.5
DSA_SCALE = DSA_HEAD_DIM ** -0.5

MLA_WIDTH = MLA_HEADS * MLA_V
DSA_WIDTH = DSA_HEADS * DSA_HEAD_DIM
MIX_WIDTH = MLA_WIDTH + DSA_WIDTH
OFF_CKV = Q_LORA
OFF_KR = OFF_CKV + KV_LORA
OFF_DQ = OFF_KR + MLA_ROPE
OFF_DK = OFF_DQ + DSA_HEADS * DSA_HEAD_DIM
OFF_DV = OFF_DK + DSA_KV_HEADS * DSA_HEAD_DIM
OFF_IQ = OFF_DV + DSA_KV_HEADS * DSA_HEAD_DIM
OFF_IK = OFF_IQ + IDX_HEADS * IDX_DIM
OFF_IW = OFF_IK + IDX_DIM
IN_COLS = OFF_IW + IDX_HEADS
IN_OFFSETS = (OFF_CKV, OFF_KR, OFF_DQ, OFF_DK, OFF_DV, OFF_IQ, OFF_IK, OFF_IW)

kernel_name = 'hymba_mla_dsa_macaron_adaln_step'


def rms_norm(x, g):
    xf = x.astype(jnp.float32)
    y = xf * lax.rsqrt(jnp.mean(xf * xf, axis=-1, keepdims=True) + EPS)
    return (y * g.astype(jnp.float32)).astype(x.dtype)


def rope(x, pos):
    half = x.shape[-1] // 2
    inv = ROPE_THETA ** (-jnp.arange(half, dtype=jnp.float32) / half)
    ang = pos.astype(jnp.float32)[:, None] * inv[None, :]
    shape = (1, pos.shape[0]) + (1,) * (x.ndim - 3) + (half,)
    cos = jnp.cos(ang).reshape(shape)
    sin = jnp.sin(ang).reshape(shape)
    x1 = x[..., :half].astype(jnp.float32)
    x2 = x[..., half:].astype(jnp.float32)
    return jnp.concatenate([x1 * cos - x2 * sin, x2 * cos + x1 * sin], axis=-1).astype(x.dtype)


def rel_bucket(dist):
    max_exact = N_BUCKETS // 2
    n = jnp.maximum(dist, 0)
    nf = jnp.maximum(n, 1).astype(jnp.float32)
    large = max_exact + (jnp.log(nf / max_exact) / math.log(MAX_DISTANCE / max_exact)
                         * (N_BUCKETS - max_exact)).astype(jnp.int32)
    large = jnp.minimum(large, N_BUCKETS - 1)
    return jnp.where(n < max_exact, n, large)


def swiglu(h, wg, wu, wd):
    return (jax.nn.silu(h @ wg) * (h @ wu)) @ wd


def gather_rows(rows, idx):
    return jax.vmap(lambda r, i: r[i])(rows, idx)


def project(h, pos, w_in, q_norm, w_uq, kv_norm):
    B, T, _ = h.shape
    cq, ckv, kr, dq, dk, dv, iq, ik, iw = jnp.split(h @ w_in, IN_OFFSETS, axis=-1)
    q = (rms_norm(cq, q_norm) @ w_uq).reshape(B, T, MLA_HEADS, MLA_NOPE + MLA_ROPE)
    q_nope = q[..., :MLA_NOPE]
    q_rope = rope(q[..., MLA_NOPE:], pos)
    ckv = rms_norm(ckv, kv_norm)
    k_rope = rope(kr, pos)
    dq = dq.reshape(B, T, DSA_HEADS, DSA_HEAD_DIM)
    dk = dk.reshape(B, T, DSA_KV_HEADS, DSA_HEAD_DIM)
    dv = dv.reshape(B, T, DSA_KV_HEADS, DSA_HEAD_DIM)
    iq = iq.reshape(B, T, IDX_HEADS, IDX_DIM)
    return q_nope, q_rope, ckv, k_rope, dq, dk, dv, iq, ik, iw


def indexer_select(iq, iw, ik, mask, k_sel):
    s = jax.nn.relu(jnp.einsum('bqhd,bkd->bhqk', iq, ik).astype(jnp.float32) * (IDX_DIM ** -0.5))
    score = jnp.einsum('bhqk,bqh->bqk', s, iw.astype(jnp.float32) * (IDX_HEADS ** -0.5))
    score = jnp.where(mask, score, -jnp.inf)
    return lax.top_k(score, k_sel)[1]


def sparse_attend(q, kg, vg, sel, qpos, rel_bias):
    B, Q = q.shape[:2]
    K = sel.shape[-1]
    qg = q.reshape(B, Q, DSA_KV_HEADS, DSA_GROUP, DSA_HEAD_DIM)
    logits = jnp.einsum('bqgnd,bqkgd->bqgnk', qg, kg).astype(jnp.float32) * DSA_SCALE
    dist = qpos[None, :, None] - sel
    bias = rel_bias[rel_bucket(dist)].reshape(B, Q, K, DSA_KV_HEADS, DSA_GROUP).transpose(0, 1, 3, 4, 2)
    valid = (dist >= 0)[:, :, None, None, :]
    logits = jnp.where(valid, logits + bias.astype(jnp.float32), -jnp.inf)
    p = jax.nn.softmax(logits, axis=-1).astype(vg.dtype)
    o = jnp.einsum('bqgnk,bqkgd->bqgnd', p, vg)
    return o.reshape(B, Q, DSA_WIDTH)


def mla_prompt(q_nope, q_rope, ckv, k_rope, w_uk, w_uv):
    B, T = q_nope.shape[:2]
    k_nope = jnp.einsum('bkr,rhd->bkhd', ckv, w_uk)
    v = jnp.einsum('bkr,rhd->bkhd', ckv, w_uv)
    kpos = jnp.arange(T, dtype=jnp.int32)

    def block(i):
        s0 = i * Q_BLOCK
        qpos = s0 + jnp.arange(Q_BLOCK, dtype=jnp.int32)
        qn = lax.dynamic_slice_in_dim(q_nope, s0, Q_BLOCK, axis=1)
        qr = lax.dynamic_slice_in_dim(q_rope, s0, Q_BLOCK, axis=1)
        logits = (jnp.einsum('bqhd,bkhd->bhqk', qn, k_nope)
                  + jnp.einsum('bqhe,bke->bhqk', qr, k_rope)).astype(jnp.float32) * MLA_SCALE
        logits = jnp.where(kpos[None, :] <= qpos[:, None], logits, -jnp.inf)
        p = jax.nn.softmax(logits, axis=-1).astype(v.dtype)
        return jnp.einsum('bhqk,bkhd->bqhd', p, v).reshape(B, Q_BLOCK, MLA_WIDTH)

    out = lax.map(block, jnp.arange(T // Q_BLOCK))
    return jnp.moveaxis(out, 0, 1).reshape(B, T, MLA_WIDTH)


def dsa_prompt(dq, dk, dv, iq, ik, iw, rel_bias):
    B, T = dq.shape[:2]
    k_sel = min(TOPK_MAX, T // 4)
    kpos = jnp.arange(T, dtype=jnp.int32)

    def block(i):
        s0 = i * Q_BLOCK
        qpos = s0 + jnp.arange(Q_BLOCK, dtype=jnp.int32)
        sl = lambda a: lax.dynamic_slice_in_dim(a, s0, Q_BLOCK, axis=1)
        sel = indexer_select(sl(iq), sl(iw), ik, kpos[None, :] <= qpos[:, None], k_sel)
        kg = gather_rows(dk, sel)
        vg = gather_rows(dv, sel)
        return sparse_attend(sl(dq), kg, vg, sel, qpos, rel_bias)

    out = lax.map(block, jnp.arange(T // Q_BLOCK))
    return jnp.moveaxis(out, 0, 1).reshape(B, T, DSA_WIDTH)


def mla_sample(q_nope, q_rope, ckv_new, kr_new, cache_ckv, cache_krope, layer, page_table, w_uk, w_uv):
    Bd, Tq = q_nope.shape[:2]
    ckv_past = cache_ckv[layer, page_table].reshape(Bd, PAST_LEN, KV_LORA).astype(ckv_new.dtype)
    kr_past = cache_krope[layer, page_table].reshape(Bd, PAST_LEN, MLA_ROPE).astype(kr_new.dtype)
    q_lat = jnp.einsum('bqhd,rhd->bqhr', q_nope, w_uk)
    lp = (jnp.einsum('bqhr,bkr->bhqk', q_lat, ckv_past)
          + jnp.einsum('bqhe,bke->bhqk', q_rope, kr_past)).astype(jnp.float32)
    ln = (jnp.einsum('bqhr,bkr->bhqk', q_lat, ckv_new)
          + jnp.einsum('bqhe,bke->bhqk', q_rope, kr_new)).astype(jnp.float32)
    causal = jnp.arange(Tq)[None, :] <= jnp.arange(Tq)[:, None]
    ln = jnp.where(causal, ln, -jnp.inf)
    p = jax.nn.softmax(jnp.concatenate([lp, ln], axis=-1) * MLA_SCALE, axis=-1).astype(ckv_new.dtype)
    o_lat = (jnp.einsum('bhqk,bkr->bqhr', p[..., :PAST_LEN], ckv_past)
             + jnp.einsum('bhqk,bkr->bqhr', p[..., PAST_LEN:], ckv_new))
    return jnp.einsum('bqhr,rhd->bqhd', o_lat, w_uv).reshape(Bd, Tq, MLA_WIDTH)


def dsa_sample(dq, dk_new, dv_new, iq, ik_new, iw, cache_k, cache_v, cache_idx_k, layer, page_table, rel_bias):
    Bd, Tq = dq.shape[:2]
    L = PAST_LEN + Tq
    k_sel = min(TOPK_MAX, L // 4)
    ik_past = cache_idx_k[layer, page_table].reshape(Bd, PAST_LEN, IDX_DIM).astype(ik_new.dtype)
    ik_all = jnp.concatenate([ik_past, ik_new], axis=1)
    qpos = PAST_LEN + jnp.arange(Tq, dtype=jnp.int32)
    kpos = jnp.arange(L, dtype=jnp.int32)
    sel = indexer_select(iq, iw, ik_all, kpos[None, :] <= qpos[:, None], k_sel)
    is_past = sel < PAST_LEN
    sp = jnp.minimum(sel, PAST_LEN - 1)
    phys = jnp.take_along_axis(page_table, (sp // PAGE_SIZE).reshape(Bd, -1), axis=1).reshape(sp.shape)
    off = sp % PAGE_SIZE
    sn = jnp.clip(sel - PAST_LEN, 0, Tq - 1)
    m = is_past[..., None, None]
    kg = jnp.where(m, cache_k[layer, phys, off].astype(dk_new.dtype), gather_rows(dk_new, sn))
    vg = jnp.where(m, cache_v[layer, phys, off].astype(dv_new.dtype), gather_rows(dv_new, sn))
    return sparse_attend(dq, kg, vg, sel, qpos, rel_bias)


def prompt_mixer(h, pos, w_in, q_norm, w_uq, kv_norm, w_uk, w_uv, rel_bias):
    q_nope, q_rope, ckv, k_rope, dq, dk, dv, iq, ik, iw = project(h, pos, w_in, q_norm, w_uq, kv_norm)
    a = mla_prompt(q_nope, q_rope, ckv, k_rope, w_uk, w_uv)
    b = dsa_prompt(dq, dk, dv, iq, ik, iw, rel_bias)
    return jnp.concatenate([a, b], axis=-1), (ckv, k_rope, dk, dv, ik)


def sample_mixer(h, pos, cache_ckv, cache_krope, cache_k, cache_v, cache_idx_k, layer, page_table,
                 w_in, q_norm, w_uq, kv_norm, w_uk, w_uv, rel_bias):
    q_nope, q_rope, ckv, k_rope, dq, dk, dv, iq, ik, iw = project(h, pos, w_in, q_norm, w_uq, kv_norm)
    a = mla_sample(q_nope, q_rope, ckv, k_rope, cache_ckv, cache_krope, layer, page_table, w_uk, w_uv)
    b = dsa_sample(dq, dk, dv, iq, ik, iw, cache_k, cache_v, cache_idx_k, layer, page_table, rel_bias)
    return jnp.concatenate([a, b], axis=-1), (ckv, k_rope, dk, dv, ik)


def trunk_layer(x, c, norm_g, w_ada, b_ada, wg, wu, wd, w_out, mixer):
    sh1, sc1, g1, sh2, sc2, g2, sh3, sc3, g3 = jnp.split((c @ w_ada + b_ada)[:, None, :], N_MOD, axis=-1)
    h = rms_norm(x, norm_g[0]) * (1 + sc1) + sh1
    x = x + 0.5 * g1 * swiglu(h, wg[0], wu[0], wd[0])
    h = rms_norm(x, norm_g[1]) * (1 + sc2) + sh2
    mix, rows = mixer(h)
    x = x + g2 * (mix @ w_out)
    h = rms_norm(x, norm_g[2]) * (1 + sc3) + sh3
    x = x + 0.5 * g3 * swiglu(h, wg[1], wu[1], wd[1])
    return x, rows


def setup_inputs(seed: int = 0) -> dict:
    key = jax.random.key(seed)
    ks = jax.random.split(key, 32)
    f32 = jnp.float32
    nrm = lambda k, shape, s=1.0: jax.random.normal(k, shape, f32) * s
    n_pages = PAST_LEN // PAGE_SIZE
    n_pool = (DEC_BATCH * n_pages * 5) // 4
    page_table = jax.random.permutation(ks[7], n_pool)[: DEC_BATCH * n_pages].reshape(DEC_BATCH, n_pages).astype(jnp.int32)
    return {
        'x_prompt': nrm(ks[0], (BATCH, SEQ, D_MODEL)),
        'x_sample': nrm(ks[1], (DEC_BATCH, DEC_SEQ, D_MODEL)),
        'cache_ckv': nrm(ks[2], (DEPTH, n_pool, PAGE_SIZE, KV_LORA)),
        'cache_krope': nrm(ks[3], (DEPTH, n_pool, PAGE_SIZE, MLA_ROPE)),
        'cache_k': nrm(ks[4], (DEPTH, n_pool, PAGE_SIZE, DSA_KV_HEADS, DSA_HEAD_DIM)),
        'cache_v': nrm(ks[5], (DEPTH, n_pool, PAGE_SIZE, DSA_KV_HEADS, DSA_HEAD_DIM)),
        'cache_idx_k': nrm(ks[6], (DEPTH, n_pool, PAGE_SIZE, IDX_DIM)),
        'page_table': page_table,
        'c_prompt': nrm(ks[8], (BATCH, D_MODEL)),
        'c_sample': nrm(ks[9], (DEC_BATCH, D_MODEL)),
        'norm_g': 1.0 + nrm(ks[10], (DEPTH, 3, D_MODEL), 0.02),
        'w_ada': nrm(ks[11], (DEPTH, D_MODEL, N_MOD * D_MODEL), 0.5 * D_MODEL ** -0.5),
        'b_ada': nrm(ks[12], (DEPTH, N_MOD * D_MODEL), 0.02),
        'ffn_w_gate': nrm(ks[13], (DEPTH, 2, D_MODEL, D_FF), D_MODEL ** -0.5),
        'ffn_w_up': nrm(ks[14], (DEPTH, 2, D_MODEL, D_FF), D_MODEL ** -0.5),
        'ffn_w_down': nrm(ks[15], (DEPTH, 2, D_FF, D_MODEL), D_FF ** -0.5),
        'w_in': nrm(ks[16], (DEPTH, D_MODEL, IN_COLS), D_MODEL ** -0.5),
        'mla_q_norm': 1.0 + nrm(ks[17], (DEPTH, Q_LORA), 0.02),
        'mla_w_uq': nrm(ks[18], (DEPTH, Q_LORA, MLA_HEADS * (MLA_NOPE + MLA_ROPE)), Q_LORA ** -0.5),
        'mla_kv_norm': 1.0 + nrm(ks[19], (DEPTH, KV_LORA), 0.02),
        'mla_w_uk': nrm(ks[20], (DEPTH, KV_LORA, MLA_HEADS, MLA_NOPE), KV_LORA ** -0.5),
        'mla_w_uv': nrm(ks[21], (DEPTH, KV_LORA, MLA_HEADS, MLA_V), KV_LORA ** -0.5),
        'w_out': nrm(ks[22], (DEPTH, MIX_WIDTH, D_MODEL), MIX_WIDTH ** -0.5),
        'rel_bias': nrm(ks[23], (N_BUCKETS, DSA_HEADS), 0.1),
        'final_norm': 1.0 + nrm(ks[24], (D_MODEL,), 0.02),
    }


def reference(x_prompt, x_sample, cache_ckv, cache_krope, cache_k, cache_v, cache_idx_k, page_table,
              c_prompt, c_sample, norm_g, w_ada, b_ada, ffn_w_gate, ffn_w_up, ffn_w_down, w_in,
              mla_q_norm, mla_w_uq, mla_kv_norm, mla_w_uk, mla_w_uv, w_out, rel_bias, final_norm):
    pos_p = jnp.arange(x_prompt.shape[1], dtype=jnp.int32)
    pos_s = PAST_LEN + jnp.arange(x_sample.shape[1], dtype=jnp.int32)
    hp, hs = x_prompt, x_sample
    rows_p, rows_s = [], []
    for l in range(DEPTH):
        lw = (norm_g[l], w_ada[l], b_ada[l], ffn_w_gate[l], ffn_w_up[l], ffn_w_down[l], w_out[l])
        mw = (w_in[l], mla_q_norm[l], mla_w_uq[l], mla_kv_norm[l], mla_w_uk[l], mla_w_uv[l], rel_bias)
        hp, rp = trunk_layer(hp, c_prompt, *lw, lambda h: prompt_mixer(h, pos_p, *mw))
        hs, rs = trunk_layer(hs, c_sample, *lw, lambda h: sample_mixer(
            h, pos_s, cache_ckv, cache_krope, cache_k, cache_v, cache_idx_k, l, page_table, *mw))
        rows_p.append(rp)
        rows_s.append(rs)
    y_prompt = rms_norm(hp, final_norm)
    y_sample = rms_norm(hs, final_norm)
    p_ckv, p_krope, p_k, p_v, p_idx_k = [jnp.stack(t, axis=0) for t in zip(*rows_p)]
    s_ckv, s_krope, s_k, s_v, s_idx_k = [jnp.stack(t, axis=0) for t in zip(*rows_s)]
    return (y_prompt, y_sample, p_ckv, p_krope, p_k, p_v, p_idx_k, s_ckv, s_krope, s_k, s_v, s_idx_k)
```

```python
import functools
import math

import numpy as np
import jax
import jax.numpy as jnp
from jax import lax
from jax.experimental import pallas as pl
from jax.experimental.pallas import tpu as pltpu

D_MODEL = 1024
MLA_HEADS = 8
MLA_NOPE = 64
MLA_ROPE = 32
MLA_V = 64
Q_LORA = 384
KV_LORA = 256
DSA_HEADS = 8
DSA_KV_HEADS = 2
DSA_HEAD_DIM = 64
DSA_GROUP = DSA_HEADS // DSA_KV_HEADS
IDX_HEADS = 8
IDX_DIM = 64
TOPK_MAX = 256
N_BUCKETS = 32
MAX_DISTANCE = 128
D_FF = 2816
N_MOD = 9
ROPE_THETA = 10000.0
EPS = 1e-6
MLA_SCALE = (MLA_NOPE + MLA_ROPE) ** -0.5
DSA_SCALE = DSA_HEAD_DIM ** -0.5
IDX_SCALE = IDX_DIM ** -0.5
IN_SPLITS = (Q_LORA, KV_LORA, MLA_ROPE, DSA_HEADS * DSA_HEAD_DIM, DSA_KV_HEADS * DSA_HEAD_DIM,
             DSA_KV_HEADS * DSA_HEAD_DIM, IDX_HEADS * IDX_DIM, IDX_DIM, IDX_HEADS)

LANES = 128
MXU_DIM = 256
VMEM_LIMIT_BYTES = 56 * 1024 * 1024

LOG2E = math.log2(math.e)
NEG = -1e30
F32_LOWEST = float(np.finfo(np.float32).min)
BF16 = jnp.bfloat16
F32 = jnp.float32

FF_CHUNK = MXU_DIM
TOKEN_TILE = 512
ATT_TILE = 256
PAGES_PER_STEP = 16


def _dot(a, b):
    return jnp.dot(a, b, preferred_element_type=F32)


def _dot_nt(a, b):
    return lax.dot_general(a, b, (((1,), (1,)), ((), ())), preferred_element_type=F32)


def _rms(x, g):
    return x * lax.rsqrt(jnp.mean(x * x, axis=-1, keepdims=True) + EPS) * g


def _params(semantics):
    return pltpu.CompilerParams(dimension_semantics=semantics, vmem_limit_bytes=VMEM_LIMIT_BYTES)


def _resident(shape):
    zeros = (0,) * len(shape)
    return pl.BlockSpec(shape, lambda *_: zeros, pipeline_mode=pl.Buffered(1))


def _adaln_body(c_ref, w_ref, b_ref, o_ref):
    o_ref[...] = _dot(c_ref[...].astype(BF16), w_ref[...].astype(BF16)) + b_ref[...]


def _adaln(c_all, w_ada, b_ada):
    depth, d, n = w_ada.shape
    r = c_all.shape[0]
    tn = 1152
    return pl.pallas_call(
        _adaln_body,
        grid=(depth, n // tn),
        in_specs=[pl.BlockSpec((r, d), lambda l, j: (0, 0)),
                  pl.BlockSpec((None, d, tn), lambda l, j: (l, 0, j)),
                  pl.BlockSpec((None, 1, tn), lambda l, j: (l, 0, j))],
        out_specs=pl.BlockSpec((None, r, tn), lambda l, j: (l, 0, j)),
        out_shape=jax.ShapeDtypeStruct((depth, r, n), F32),
        compiler_params=_params(("arbitrary", "arbitrary")),
        name="adaln",
    )(c_all, w_ada, b_ada.reshape(depth, 1, n))


def _ffn_body(*refs, with_mix, with_final):
    x_ref, sh_ref, sc_ref, g_ref, ng_ref, wg_ref, wu_ref, wd_ref = refs[:8]
    rest = list(refs[8:])
    if with_mix:
        a_ref, b_ref, wo_ref, g2_ref = rest[:4]
        rest = rest[4:]
    if with_final:
        fg_ref = rest[0]
        rest = rest[1:]
    o_ref = rest[0]

    x = x_ref[...]
    if with_mix:
        half = a_ref.shape[-1]
        mix = _dot(a_ref[...], wo_ref[:half, :]) + _dot(b_ref[...], wo_ref[half:, :])
        x = x + g2_ref[...] * mix
    h = (_rms(x, ng_ref[...]) * (1.0 + sc_ref[...]) + sh_ref[...]).astype(BF16)
    acc = jnp.zeros(x.shape, F32)
    for c in range(wg_ref.shape[0]):
        gate = _dot(h, wg_ref[c])
        up = _dot(h, wu_ref[c])
        act = (gate * (1.0 / (1.0 + jnp.exp(-gate))) * up).astype(BF16)
        acc = acc + _dot(act, wd_ref[c])
    y = x + 0.5 * g_ref[...] * acc
    if with_final:
        y = _rms(y, fg_ref[...])
    o_ref[...] = y


def _ffn(x, mod, chunk0, norm_g, wg, wu, wd, mix=None, final_g=None):
    b, t, d = x.shape
    r = mod.shape[1]
    tm = min(TOKEN_TILE, t)
    rm = 1 if r == 1 else tm

    def mod_spec(k):
        if r == 1:
            return pl.BlockSpec((None, 1, d), lambda bi, i: (bi, 0, k))
        return pl.BlockSpec((None, rm, d), lambda bi, i: (bi, i, k))

    tok = lambda w: pl.BlockSpec((None, tm, w), lambda bi, i: (bi, i, 0))
    in_specs = [tok(d), mod_spec(chunk0), mod_spec(chunk0 + 1), mod_spec(chunk0 + 2),
                _resident((1, d)), _resident(wg.shape), _resident(wu.shape), _resident(wd.shape)]
    args = [x, mod, mod, mod, norm_g.reshape(1, d), wg, wu, wd]
    if mix is not None:
        a, bb, wo, g2_chunk = mix
        in_specs += [tok(a.shape[-1]), tok(bb.shape[-1]), _resident(wo.shape), mod_spec(g2_chunk)]
        args += [a, bb, wo, mod]
    if final_g is not None:
        in_specs.append(_resident((1, d)))
        args.append(final_g.reshape(1, d))
    return pl.pallas_call(
        functools.partial(_ffn_body, with_mix=mix is not None, with_final=final_g is not None),
        grid=(b, t // tm),
        in_specs=in_specs,
        out_specs=tok(d),
        out_shape=jax.ShapeDtypeStruct((b, t, d), F32),
        compiler_params=_params(("arbitrary", "arbitrary")),
        name="ffn",
    )(*args)


P_CQ = 0
P_CKV = P_CQ + Q_LORA
P_KR = P_CKV + KV_LORA
P_KRR = P_KR + LANES
P_DQ = P_KRR + LANES
P_DK = P_DQ + DSA_HEADS * DSA_HEAD_DIM
P_DV = P_DK + LANES
P_IQ = P_DV + LANES
P_IK = P_IQ + IDX_HEADS * IDX_DIM
P_IW = P_IK + LANES
P_COLS = P_IW + LANES
HEAD_TILE = LANES
ROPE_LANE0 = MLA_NOPE


def _proj_body(x_ref, sh_ref, sc_ref, ng_ref, w_ref, qn_ref, wab_ref, kvn_ref, wuk_ref, wuv_ref,
               cos_ref, sin_ref,
               q_ref, k_ref, v_ref, ckv_ref, kr_ref, dqp_ref, dk_ref, dv_ref, dkb_ref, dvb_ref,
               iq_ref, ik_ref, ikd_ref, iw_ref):
    h = (_rms(x_ref[...], ng_ref[...]) * (1.0 + sc_ref[...]) + sh_ref[...]).astype(BF16)
    cos = cos_ref[...]
    sin = sin_ref[...]
    nq = MLA_HEADS * HEAD_TILE

    cq = _rms(_dot(h, w_ref[:, P_CQ:P_CKV]), qn_ref[...]).astype(BF16)
    ab = _dot(cq, wab_ref[...])
    for hh in range(MLA_HEADS):
        lo = hh * HEAD_TILE
        q_ref[:, lo:lo + HEAD_TILE] = (ab[:, lo:lo + HEAD_TILE] * cos
                                       + ab[:, nq + lo:nq + lo + HEAD_TILE] * sin).astype(BF16)

    ckv = _rms(_dot(h, w_ref[:, P_CKV:P_KR]), kvn_ref[...])
    ckv_ref[...] = ckv
    ckv_b = ckv.astype(BF16)
    kr2 = _dot(h, w_ref[:, P_KR:P_DQ])
    krt = kr2[:, :LANES] * cos + kr2[:, LANES:] * sin
    kr_ref[...] = pltpu.roll(krt, LANES - ROPE_LANE0, axis=1)[:, :MLA_ROPE]
    kn = _dot(ckv_b, wuk_ref[...])
    for hh in range(MLA_HEADS):
        lo = hh * HEAD_TILE
        k_ref[:, lo:lo + HEAD_TILE] = (kn[:, lo:lo + HEAD_TILE] + krt).astype(BF16)
    v_ref[...] = _dot(ckv_b, wuv_ref[...]).astype(BF16)

    dq = _dot(h, w_ref[:, P_DQ:P_DK]) * DSA_SCALE
    lane = lax.broadcasted_iota(jnp.int32, (dq.shape[0], LANES), 1)
    low = lane < DSA_HEAD_DIM
    for j in range(DSA_HEADS // 2):
        t = dq[:, j * LANES:(j + 1) * LANES]
        tr = pltpu.roll(t, DSA_HEAD_DIM, axis=1)
        g = (2 * j) // DSA_GROUP
        even = jnp.where(low, t, 0.0) if g == 0 else jnp.where(low, 0.0, tr)
        odd = jnp.where(low, tr, 0.0) if g == 0 else jnp.where(low, 0.0, t)
        dqp_ref[:, (2 * j) * LANES:(2 * j + 1) * LANES] = even.astype(BF16)
        dqp_ref[:, (2 * j + 1) * LANES:(2 * j + 2) * LANES] = odd.astype(BF16)

    dkv = _dot(h, w_ref[:, P_DK:P_IQ])
    dk_ref[...] = dkv[:, :LANES]
    dv_ref[...] = dkv[:, LANES:]
    dkb_ref[...] = dkv[:, :LANES].astype(BF16)
    dvb_ref[...] = dkv[:, LANES:].astype(BF16)

    iq_ref[...] = _dot(h, w_ref[:, P_IQ:P_IK]).astype(BF16)
    ik2 = _dot(h, w_ref[:, P_IK:P_COLS])
    ik_ref[...] = ik2[:, :IDX_DIM]
    ikd_ref[...] = ik2[:, :LANES].astype(BF16)
    iw_ref[...] = ik2[:, LANES:] * (IDX_HEADS ** -0.5) * IDX_SCALE


def _proj(x, mod, norm_g, lw, cos_t, sin_t):
    b, t, d = x.shape
    r = mod.shape[1]
    tm = min(TOKEN_TILE, t)

    def mod_spec(k):
        if r == 1:
            return pl.BlockSpec((None, 1, d), lambda bi, i: (bi, 0, k))
        return pl.BlockSpec((None, tm, d), lambda bi, i: (bi, i, k))

    tok = lambda w: pl.BlockSpec((None, tm, w), lambda bi, i: (bi, i, 0))
    tab = pl.BlockSpec((tm, LANES), lambda bi, i: (i, 0))
    in_specs = [tok(d), mod_spec(3), mod_spec(4), _resident((1, d)), _resident(lw["w_in"].shape),
                _resident((1, Q_LORA)), _resident(lw["w_ab"].shape), _resident((1, KV_LORA)),
                _resident(lw["w_ukp"].shape), _resident(lw["w_uv"].shape), tab, tab]
    nq = MLA_HEADS * HEAD_TILE
    outs = [("q", nq, BF16), ("k", nq, BF16), ("v", MLA_HEADS * MLA_V, BF16), ("ckv", KV_LORA, F32),
            ("kr", MLA_ROPE, F32), ("dqp", DSA_HEADS * LANES, BF16), ("dk", LANES, F32), ("dv", LANES, F32),
            ("dkb", LANES, BF16), ("dvb", LANES, BF16), ("iq", IDX_HEADS * IDX_DIM, BF16),
            ("ik", IDX_DIM, F32), ("ikd", LANES, BF16), ("iw", LANES, F32)]
    res = pl.pallas_call(
        _proj_body,
        grid=(b, t // tm),
        in_specs=in_specs,
        out_specs=[tok(w) for _, w, _ in outs],
        out_shape=[jax.ShapeDtypeStruct((b, t, w), dt) for _, w, dt in outs],
        compiler_params=_params(("arbitrary", "arbitrary")),
        name="proj",
    )(x, mod, mod, norm_g.reshape(1, d), lw["w_in"], lw["q_norm"], lw["w_ab"], lw["kv_norm"],
      lw["w_ukp"], lw["w_uv"], cos_t, sin_t)
    return {name: val for (name, _, _), val in zip(outs, res)}


def _bucket_table(max_d):
    n = np.arange(max_d + 1, dtype=np.int32)
    max_exact = N_BUCKETS // 2
    nf = np.maximum(n, 1).astype(np.float32)
    large = max_exact + (np.log(nf / np.float32(max_exact)) / np.float32(math.log(MAX_DISTANCE / max_exact))
                         * np.float32(N_BUCKETS - max_exact)).astype(np.int32)
    large = np.minimum(large, N_BUCKETS - 1)
    return np.where(n < max_exact, n, large).astype(np.int32)


def _bias_body(rb_ref, bkt_ref, bkts_ref, bp_ref, bs_ref):
    h = pl.program_id(0)
    bkt = bkt_ref[...]
    bkts = bkts_ref[...]
    tile = jnp.zeros(bkt.shape, F32)
    row = jnp.zeros(bkts.shape, F32)
    for bi in range(N_BUCKETS):
        val = rb_ref[bi, h] * LOG2E
        tile = jnp.where(bkt == bi, val, tile)
        row = jnp.where(bkts == bi, val, row)
    bp_ref[...] = tile
    bs_ref[...] = row


def _bias_tables(rel_bias, tq, tk):
    table = _bucket_table(tq + tk)
    r = np.arange(tq)[:, None]
    c = np.arange(tk)[None, :]
    bkt_p = np.stack([table[np.clip(r - c, 0, None)], table[tq + r - c]]).astype(np.int32)
    d_last = LANES - np.arange(LANES)
    bkt_s = np.stack([table[np.minimum(d_last, tq + tk)], np.full(LANES, N_BUCKETS - 1),
                      np.zeros(LANES, np.int64)]).astype(np.int32)
    return pl.pallas_call(
        _bias_body,
        grid=(DSA_HEADS,),
        in_specs=[pl.BlockSpec(memory_space=pltpu.SMEM),
                  pl.BlockSpec(bkt_p.shape, lambda h: (0, 0, 0)),
                  pl.BlockSpec(bkt_s.shape, lambda h: (0, 0))],
        out_specs=[pl.BlockSpec((None,) + bkt_p.shape, lambda h: (h, 0, 0, 0)),
                   pl.BlockSpec((None,) + bkt_s.shape, lambda h: (h, 0, 0))],
        out_shape=[jax.ShapeDtypeStruct((DSA_HEADS,) + bkt_p.shape, F32),
                   jax.ShapeDtypeStruct((DSA_HEADS,) + bkt_s.shape, F32)],
        compiler_params=_params(("arbitrary",)),
        name="bias_tables",
    )(rel_bias, jnp.asarray(bkt_p), jnp.asarray(bkt_s))


def _pair_step(ts, vt, m_ref, l_ref, acc_ref, low):
    outs, alphas = [], []
    for hh in range(2):
        t = ts[hh]
        m_old = m_ref[hh]
        m_new = jnp.maximum(m_old, jnp.max(t, axis=-1, keepdims=True))
        p = jnp.exp2(t - m_new)
        alpha = jnp.exp2(m_old - m_new)
        l_ref[hh] = alpha * l_ref[hh] + jnp.sum(p, axis=-1, keepdims=True)
        m_ref[hh] = m_new
        outs.append(_dot(p.astype(BF16), vt))
        alphas.append(alpha)
    acc_ref[...] = acc_ref[...] * jnp.where(low, alphas[0], alphas[1]) + jnp.where(low, outs[0], outs[1])


def _pair_init(m_ref, l_ref, acc_ref):
    m_ref[...] = jnp.full(m_ref.shape, -jnp.inf, F32)
    l_ref[...] = jnp.zeros(l_ref.shape, F32)
    acc_ref[...] = jnp.zeros(acc_ref.shape, F32)


def _pair_out(l_ref, acc_ref, low):
    return (acc_ref[...] * jnp.where(low, 1.0 / l_ref[0], 1.0 / l_ref[1])).astype(BF16)


def _mla_prompt_body(q_ref, k_ref, v_ref, o_ref, m_ref, l_ref, acc_ref, *, tile):
    i = pl.program_id(1)
    low = lax.broadcasted_iota(jnp.int32, (tile, LANES), 1) < MLA_V
    row = lax.broadcasted_iota(jnp.int32, (tile, tile), 0)
    col = lax.broadcasted_iota(jnp.int32, (tile, tile), 1)
    scale = MLA_SCALE * LOG2E
    for pair in range(MLA_HEADS // 2):
        _pair_init(m_ref, l_ref, acc_ref)
        q0 = q_ref[:, (2 * pair) * HEAD_TILE:(2 * pair + 1) * HEAD_TILE]
        q1 = q_ref[:, (2 * pair + 1) * HEAD_TILE:(2 * pair + 2) * HEAD_TILE]

        def step(kb, causal, pair=pair, q0=q0, q1=q1):
            ks = pl.ds(pl.multiple_of(kb * tile, tile), tile)
            ts = []
            for hh, q in enumerate((q0, q1)):
                lo = (2 * pair + hh) * HEAD_TILE
                t = _dot_nt(q, k_ref[ks, lo:lo + HEAD_TILE]) * scale
                if causal:
                    t = jnp.where(col <= row, t, -jnp.inf)
                ts.append(t)
            _pair_step(ts, v_ref[ks, pair * LANES:(pair + 1) * LANES], m_ref, l_ref, acc_ref, low)

        def off_diag(kb, carry):
            step(kb, False)
            return carry

        lax.fori_loop(0, i, off_diag, 0)
        step(i, True)
        o_ref[:, pair * LANES:(pair + 1) * LANES] = _pair_out(l_ref, acc_ref, low)


def _mla_prompt(q, k, v):
    b, t, nq = q.shape
    tile = min(ATT_TILE, t)
    nv = v.shape[-1]
    return pl.pallas_call(
        functools.partial(_mla_prompt_body, tile=tile),
        grid=(b, t // tile),
        in_specs=[pl.BlockSpec((None, tile, nq), lambda bi, i: (bi, i, 0)),
                  pl.BlockSpec((None, t, nq), lambda bi, i: (bi, 0, 0)),
                  pl.BlockSpec((None, t, nv), lambda bi, i: (bi, 0, 0))],
        out_specs=pl.BlockSpec((None, tile, nv), lambda bi, i: (bi, i, 0)),
        out_shape=jax.ShapeDtypeStruct((b, t, nv), BF16),
        scratch_shapes=[pltpu.VMEM((2, tile, 1), F32), pltpu.VMEM((2, tile, 1), F32),
                        pltpu.VMEM((tile, LANES), F32)],
        compiler_params=_params(("arbitrary", "arbitrary")),
        name="mla_prompt",
    )(q, k, v)


INT_MIN = -2 ** 31


def _key_to_f32(key):
    bits = jnp.where(key < 0, key ^ jnp.int32(0x7FFFFFFF), key)
    return lax.bitcast_convert_type(bits, F32)


def _kth_largest(count_ge, k, rows):
    cnt = count_ge(jnp.zeros((rows, 1), F32))
    cur = jnp.where(cnt >= k, jnp.int32(0), jnp.int32(INT_MIN))

    def body(it, cur):
        cand = cur + lax.shift_left(jnp.int32(1), jnp.int32(30) - it)
        cnt = count_ge(_key_to_f32(cand))
        return jnp.where(cnt >= k, cand, cur)

    cur = lax.fori_loop(0, 31, body, cur)
    return _key_to_f32(cur)


def _tie_cut(count_eq_below, need, n_cols, rows):
    bits = max(1, int(math.ceil(math.log2(n_cols))))

    def body(it, cur):
        cand = cur + lax.shift_left(jnp.int32(1), jnp.int32(bits - 1) - it)
        return jnp.where(count_eq_below(cand) < need, cand, cur)

    return lax.fori_loop(0, bits, body, jnp.zeros((rows, 1), jnp.int32))


def _dsa_prompt_body(iq_ref, iw_ref, dqp_ref, ikd_ref, dkb_ref, dvb_ref, bias_ref, o_ref,
                     s_ref, iqp_ref, wb_ref, m_ref, l_ref, acc_ref, *, tile, k_sel):
    i = pl.program_id(1)
    lane = lax.broadcasted_iota(jnp.int32, (tile, LANES), 1)
    low = lane < DSA_HEAD_DIM
    row = lax.broadcasted_iota(jnp.int32, (tile, tile), 0)
    col = lax.broadcasted_iota(jnp.int32, (tile, tile), 1)
    reps = tile // LANES

    for h in range(IDX_HEADS):
        t = iq_ref[:, (h // 2) * LANES:(h // 2 + 1) * LANES]
        keep = low if h % 2 == 0 else jnp.logical_not(low)
        iqp_ref[h] = jnp.where(keep, t, jnp.zeros_like(t))
        wb_ref[h] = jnp.broadcast_to(iw_ref[:, h:h + 1], (tile, LANES))

    def score_block(kb, causal):
        ks = pl.ds(pl.multiple_of(kb * tile, tile), tile)
        kt = ikd_ref[ks, :]
        sc = jnp.zeros((tile, tile), F32)
        for h in range(IDX_HEADS):
            w = jnp.concatenate([wb_ref[h]] * reps, axis=1)
            sc = sc + jnp.maximum(_dot_nt(iqp_ref[h], kt), 0.0) * w
        if causal:
            sc = jnp.where(col <= row, sc, -jnp.inf)
        s_ref[kb] = sc

    def score_loop(kb, carry):
        score_block(kb, False)
        return carry

    lax.fori_loop(0, i, score_loop, 0)
    score_block(i, True)

    def lane_fold(x):
        out = x[:, :LANES]
        for r in range(1, reps):
            out = out + x[:, r * LANES:(r + 1) * LANES]
        return out

    def masked_count(pred):
        def body(kb, acc):
            return acc + lane_fold(jnp.where(pred(s_ref[kb], kb), 1.0, 0.0))
        acc = lax.fori_loop(0, i + 1, body, jnp.zeros((tile, LANES), F32))
        return jnp.sum(acc, axis=-1, keepdims=True).astype(jnp.int32)

    n_adm = i * tile + lax.broadcasted_iota(jnp.int32, (tile, 1), 0) + 1
    thr = _kth_largest(lambda cand: masked_count(lambda s, kb: s >= cand), k_sel, tile)
    thr = jnp.where(n_adm <= k_sel, F32_LOWEST, thr)
    n_gt = masked_count(lambda s, kb: s > thr)
    need = k_sel - n_gt
    cut = _tie_cut(lambda p: masked_count(lambda s, kb: (s == thr) & (kb * tile + col < p)),
                   need, s_ref.shape[0] * tile, tile)

    def mask_loop(kb, carry):
        s = s_ref[kb]
        sel = (s > thr) | ((s == thr) & (kb * tile + col <= cut))
        s_ref[kb] = jnp.where(sel, 0.0, NEG)
        return carry

    lax.fori_loop(0, i + 1, mask_loop, 0)

    for h in range(DSA_GROUP):
        heads = (h, h + DSA_GROUP)
        _pair_init(m_ref, l_ref, acc_ref)
        qs = [dqp_ref[:, hd * LANES:(hd + 1) * LANES] for hd in heads]

        def step(kb, kind, heads=heads, qs=qs):
            ks = pl.ds(pl.multiple_of(kb * tile, tile), tile)
            kt = dkb_ref[ks, :]
            mb = s_ref[kb]
            ts = []
            for hd, q in zip(heads, qs):
                if kind == "far":
                    bias = bias_ref[hd, 1, tile - 1:tile, 0:1]
                else:
                    bias = bias_ref[hd, 0 if kind == "diag" else 1]
                ts.append(_dot_nt(q, kt) * LOG2E + (mb + bias))
            _pair_step(ts, dvb_ref[ks, :], m_ref, l_ref, acc_ref, low)

        def far_loop(kb, carry):
            step(kb, "far")
            return carry

        lax.fori_loop(0, jnp.maximum(i - 1, 0), far_loop, 0)

        @pl.when(i >= 1)
        def _():
            step(i - 1, "prev")

        step(i, "diag")
        o_ref[:, h * LANES:(h + 1) * LANES] = _pair_out(l_ref, acc_ref, low)


def _dsa_prompt(pr, bias_p, k_sel):
    iq, iw, dqp, ikd, dkb, dvb = pr["iq"], pr["iw"], pr["dqp"], pr["ikd"], pr["dkb"], pr["dvb"]
    b, t, _ = iq.shape
    tile = min(ATT_TILE, t)
    assert bias_p.shape[-1] == tile and tile >= MAX_DISTANCE
    blk = lambda w: pl.BlockSpec((None, tile, w), lambda bi, i: (bi, i, 0))
    seq = lambda w: pl.BlockSpec((None, t, w), lambda bi, i: (bi, 0, 0))
    return pl.pallas_call(
        functools.partial(_dsa_prompt_body, tile=tile, k_sel=k_sel),
        grid=(b, t // tile),
        in_specs=[blk(iq.shape[-1]), blk(LANES), blk(dqp.shape[-1]), seq(LANES), seq(LANES), seq(LANES),
                  _resident(bias_p.shape)],
        out_specs=blk(DSA_HEADS * DSA_HEAD_DIM),
        out_shape=jax.ShapeDtypeStruct((b, t, DSA_HEADS * DSA_HEAD_DIM), BF16),
        scratch_shapes=[pltpu.VMEM((t // tile, tile, tile), F32),
                        pltpu.VMEM((IDX_HEADS, tile, LANES), BF16),
                        pltpu.VMEM((IDX_HEADS, tile, LANES), F32),
                        pltpu.VMEM((2, tile, 1), F32), pltpu.VMEM((2, tile, 1), F32),
                        pltpu.VMEM((tile, LANES), F32)],
        compiler_params=_params(("arbitrary", "arbitrary")),
        name="dsa_prompt",
    )(iq, iw, dqp, ikd, dkb, dvb, bias_p)


def _sample_q_body(q_ref, wukt_ref, ql_ref, qr_ref):
    for h in range(MLA_HEADS):
        qt = q_ref[:, h * HEAD_TILE:(h + 1) * HEAD_TILE]
        ql_ref[:, h * KV_LORA:(h + 1) * KV_LORA] = _dot(qt, wukt_ref[h]).astype(BF16)
        qr = pltpu.roll(qt.astype(F32), LANES - ROPE_LANE0, axis=1)[:, :MLA_ROPE]
        qr_ref[:, h * MLA_ROPE:(h + 1) * MLA_ROPE] = qr.astype(BF16)


def _sample_q(q, w_ukt):
    n = q.shape[0]
    return pl.pallas_call(
        _sample_q_body,
        out_shape=[jax.ShapeDtypeStruct((n, MLA_HEADS * KV_LORA), BF16),
                   jax.ShapeDtypeStruct((n, MLA_HEADS * MLA_ROPE), BF16)],
        compiler_params=pltpu.CompilerParams(vmem_limit_bytes=VMEM_LIMIT_BYTES),
        name="sample_q",
    )(q, w_ukt)


def _decode1_body(pt_ref, *refs, g_pages, scale):
    del pt_ref
    ckv = refs[:g_pages]
    krp = refs[g_pages:2 * g_pages]
    ikp = refs[2 * g_pages:3 * g_pages]
    ql_ref, qr_ref, cn_ref, kn_ref, iq_ref, iw_ref, o_ref, sc_ref, m_ref, l_ref, acc_ref = refs[3 * g_pages:]
    c = pl.program_id(1)
    nc = pl.num_programs(1)

    @pl.when(c == 0)
    def _():
        m_ref[...] = jnp.full(m_ref.shape, -jnp.inf, F32)
        l_ref[...] = jnp.zeros(l_ref.shape, F32)
        acc_ref[...] = jnp.zeros(acc_ref.shape, F32)

    ql = ql_ref[...].astype(F32)
    qr = qr_ref[...].astype(F32)
    iq = iq_ref[...].astype(F32)
    iw = iw_ref[...]
    logits, scores = [], []
    for g in range(g_pages):
        logits.append(_dot_nt(ql, ckv[g][...]) + _dot(qr, krp[g][...]))
        s = jnp.maximum(_dot(iq, ikp[g][...]), 0.0) * iw
        scores.append(jnp.sum(s, axis=0, keepdims=True))
    sc_ref[...] = jnp.concatenate(scores, axis=1)
    t = jnp.concatenate(logits, axis=1) * scale
    m_old = m_ref[...]
    m_new = jnp.maximum(m_old, jnp.max(t, axis=-1, keepdims=True))
    p = jnp.exp2(t - m_new)
    alpha = jnp.exp2(m_old - m_new)
    l_ref[...] = alpha * l_ref[...] + jnp.sum(p, axis=-1, keepdims=True)
    m_ref[...] = m_new
    pv = jnp.zeros(acc_ref.shape, F32)
    for g in range(g_pages):
        pv = pv + _dot(p[:, g * LANES:(g + 1) * LANES], ckv[g][...])
    acc_ref[...] = acc_ref[...] * alpha + pv

    @pl.when(c == nc - 1)
    def _():
        cn = cn_ref[...].astype(BF16).astype(F32)
        kn = kn_ref[...].astype(BF16).astype(F32)
        ln = (jnp.sum(ql * cn, axis=-1, keepdims=True) + jnp.sum(qr * kn, axis=-1, keepdims=True)) * scale
        m_old = m_ref[...]
        m_new = jnp.maximum(m_old, ln)
        pn = jnp.exp2(ln - m_new)
        alpha = jnp.exp2(m_old - m_new)
        l = alpha * l_ref[...] + pn
        acc = acc_ref[...] * alpha + pn.astype(BF16).astype(F32) * cn
        o_ref[...] = (acc / l).astype(BF16)


def _decode1(layer, page_table, cache_ckv, cache_krope, cache_idx_k, ql, qr, ckv_new, kr_new, iq, iw):
    bd, n_pages = page_table.shape
    g_pages = min(PAGES_PER_STEP, n_pages)
    nc = n_pages // g_pages
    gw = g_pages * LANES

    def page_spec(rows, width, g):
        return pl.BlockSpec((None, None, rows, width),
                            lambda b, c, pt: (layer, pt[b, c * g_pages + g], 0, 0))

    per_seq = lambda r, w: pl.BlockSpec((None, r, w), lambda b, c, pt: (b, 0, 0))
    in_specs = ([page_spec(LANES, KV_LORA, g) for g in range(g_pages)]
                + [page_spec(MLA_ROPE, LANES, g) for g in range(g_pages)]
                + [page_spec(IDX_DIM, LANES, g) for g in range(g_pages)]
                + [per_seq(MLA_HEADS, KV_LORA), per_seq(MLA_HEADS, MLA_ROPE), per_seq(1, KV_LORA),
                   per_seq(1, MLA_ROPE), per_seq(IDX_HEADS, IDX_DIM), per_seq(IDX_HEADS, 1)])
    grid_spec = pltpu.PrefetchScalarGridSpec(
        num_scalar_prefetch=1,
        grid=(bd, nc),
        in_specs=in_specs,
        out_specs=[per_seq(MLA_HEADS, KV_LORA),
                   pl.BlockSpec((None, None, 1, gw), lambda b, c, pt: (c, b, 0, 0))],
        scratch_shapes=[pltpu.VMEM((MLA_HEADS, 1), F32), pltpu.VMEM((MLA_HEADS, 1), F32),
                        pltpu.VMEM((MLA_HEADS, KV_LORA), F32)],
    )
    o_lat, scores = pl.pallas_call(
        functools.partial(_decode1_body, g_pages=g_pages, scale=MLA_SCALE * LOG2E),
        grid_spec=grid_spec,
        out_shape=[jax.ShapeDtypeStruct((bd, MLA_HEADS, KV_LORA), BF16),
                   jax.ShapeDtypeStruct((nc, bd, 1, gw), F32)],
        compiler_params=_params(("arbitrary", "arbitrary")),
        name="decode_mla_idx",
    )(page_table, *([cache_ckv] * g_pages), *([cache_krope] * g_pages), *([cache_idx_k] * g_pages),
      ql, qr, ckv_new, kr_new, iq, iw)
    return o_lat, scores.reshape(nc, bd, gw)


def _sample_topk_body(s_ref, iq_ref, ikd_ref, iw_ref, mb_ref, mbn_ref, *, k_sel):
    nc, rows, gw = s_ref.shape
    col = lax.broadcasted_iota(jnp.int32, (rows, gw), 1)
    reps = gw // LANES

    prod = iq_ref[...].astype(F32) * jnp.concatenate([ikd_ref[...].astype(F32)] * (IDX_HEADS // 2), axis=1)
    s_new = jnp.zeros((rows, 1), F32)
    for h in range(IDX_HEADS):
        dot_h = jnp.sum(prod[:, h * IDX_DIM:(h + 1) * IDX_DIM], axis=-1, keepdims=True)
        s_new = s_new + jnp.maximum(dot_h, 0.0) * iw_ref[:, h:h + 1]

    def lane_fold(x):
        out = x[:, :LANES]
        for r in range(1, reps):
            out = out + x[:, r * LANES:(r + 1) * LANES]
        return out

    def masked_count(pred, pred_new):
        acc = jnp.zeros((rows, LANES), F32)
        for c in range(nc):
            acc = acc + lane_fold(jnp.where(pred(s_ref[c], c), 1.0, 0.0))
        cnt = jnp.sum(acc, axis=-1, keepdims=True) + jnp.where(pred_new, 1.0, 0.0)
        return cnt.astype(jnp.int32)

    n_cols = nc * gw
    thr = _kth_largest(lambda cand: masked_count(lambda s, c: s >= cand, s_new >= cand), k_sel, rows)
    need = k_sel - masked_count(lambda s, c: s > thr, s_new > thr)
    cut = _tie_cut(lambda p: masked_count(lambda s, c: (s == thr) & (c * gw + col < p),
                                          (s_new == thr) & (n_cols < p)),
                   need, n_cols + 1, rows)
    for c in range(nc):
        s = s_ref[c]
        sel = (s > thr) | ((s == thr) & (c * gw + col <= cut))
        mb_ref[c] = jnp.where(sel, 0.0, NEG)
    sel_new = (s_new > thr) | ((s_new == thr) & (n_cols <= cut))
    mbn_ref[...] = jnp.broadcast_to(jnp.where(sel_new, 0.0, NEG), mbn_ref.shape)


def _sample_topk(scores, iq, ikd, iw, k_sel):
    nc, bd, gw = scores.shape
    rows = min(32, bd)
    return pl.pallas_call(
        functools.partial(_sample_topk_body, k_sel=k_sel),
        grid=(bd // rows,),
        in_specs=[pl.BlockSpec((nc, rows, gw), lambda r: (0, r, 0)),
                  pl.BlockSpec((rows, iq.shape[-1]), lambda r: (r, 0)),
                  pl.BlockSpec((rows, LANES), lambda r: (r, 0)),
                  pl.BlockSpec((rows, LANES), lambda r: (r, 0))],
        out_specs=[pl.BlockSpec((nc, rows, gw), lambda r: (0, r, 0)),
                   pl.BlockSpec((rows, LANES), lambda r: (r, 0))],
        out_shape=[jax.ShapeDtypeStruct((nc, bd, gw), F32), jax.ShapeDtypeStruct((bd, LANES), F32)],
        compiler_params=_params(("arbitrary",)),
        name="sample_topk",
    )(scores, iq, ikd, iw)


def _decode2_body(pt_ref, *refs, g_pages):
    del pt_ref
    kp = refs[:g_pages]
    vp = refs[g_pages:2 * g_pages]
    q_ref, mb_ref, mbn_ref, bias_ref, dkn_ref, dvn_ref, o_ref, m_ref, l_ref, acc_ref = refs[2 * g_pages:]
    c = pl.program_id(1)
    nc = pl.num_programs(1)

    @pl.when(c == 0)
    def _():
        m_ref[...] = jnp.full(m_ref.shape, -jnp.inf, F32)
        l_ref[...] = jnp.zeros(l_ref.shape, F32)
        acc_ref[...] = jnp.zeros(acc_ref.shape, F32)

    q = q_ref[...].astype(F32)
    mb = mb_ref[...]
    far = bias_ref[1]
    last = jnp.where(c == nc - 1, bias_ref[0], far)
    logits = []
    for g in range(g_pages):
        bias = last if g == g_pages - 1 else far
        logits.append(_dot(q, kp[g][...]) * LOG2E + (bias + mb[:, g * LANES:(g + 1) * LANES]))
    t = jnp.concatenate(logits, axis=1)
    m_old = m_ref[...]
    m_new = jnp.maximum(m_old, jnp.max(t, axis=-1, keepdims=True))
    p = jnp.exp2(t - m_new)
    alpha = jnp.exp2(m_old - m_new)
    l_ref[...] = alpha * l_ref[...] + jnp.sum(p, axis=-1, keepdims=True)
    m_ref[...] = m_new
    pv = jnp.zeros(acc_ref.shape, F32)
    for g in range(g_pages):
        pv = pv + _dot_nt(p[:, g * LANES:(g + 1) * LANES], vp[g][...])
    acc_ref[...] = acc_ref[...] * alpha + pv

    @pl.when(c == nc - 1)
    def _():
        dkn = dkn_ref[...].astype(BF16).astype(F32)
        dvn = dvn_ref[...].astype(BF16).astype(F32)
        ln = (jnp.sum(q * dkn, axis=-1, keepdims=True) * LOG2E
              + (bias_ref[2][:, 0:1] + mbn_ref[:, 0:1]))
        m_old = m_ref[...]
        m_new = jnp.maximum(m_old, ln)
        pn = jnp.exp2(ln - m_new)
        alpha = jnp.exp2(m_old - m_new)
        l = alpha * l_ref[...] + pn
        out = (acc_ref[...] * alpha + pn.astype(BF16).astype(F32) * dvn) / l
        rowi = lax.broadcasted_iota(jnp.int32, out.shape, 0)
        out = jnp.where(rowi < DSA_GROUP, out, pltpu.roll(out, DSA_HEAD_DIM, axis=1))
        o_ref[...] = out[:, :DSA_HEAD_DIM].astype(BF16)


def _decode2(layer, page_table, cache_k, cache_v, dqp, mb, mbn, bias_s, dk_new, dv_new):
    bd, n_pages = page_table.shape
    g_pages = min(PAGES_PER_STEP, n_pages)
    nc = n_pages // g_pages
    gw = g_pages * LANES

    def page_spec(g):
        return pl.BlockSpec((None, None, LANES, LANES),
                            lambda b, c, pt: (layer, pt[b, c * g_pages + g], 0, 0))

    per_seq = lambda r, w: pl.BlockSpec((None, r, w), lambda b, c, pt: (b, 0, 0))
    in_specs = ([page_spec(g) for g in range(g_pages)] + [page_spec(g) for g in range(g_pages)]
                + [per_seq(DSA_HEADS, LANES),
                   pl.BlockSpec((None, None, 1, gw), lambda b, c, pt: (c, b, 0, 0)),
                   per_seq(1, LANES),
                   pl.BlockSpec(bias_s.shape, lambda b, c, pt: (0, 0, 0)),
                   per_seq(1, LANES), per_seq(1, LANES)])
    grid_spec = pltpu.PrefetchScalarGridSpec(
        num_scalar_prefetch=1,
        grid=(bd, nc),
        in_specs=in_specs,
        out_specs=per_seq(DSA_HEADS, DSA_HEAD_DIM),
        scratch_shapes=[pltpu.VMEM((DSA_HEADS, 1), F32), pltpu.VMEM((DSA_HEADS, 1), F32),
                        pltpu.VMEM((DSA_HEADS, LANES), F32)],
    )
    return pl.pallas_call(
        functools.partial(_decode2_body, g_pages=g_pages),
        grid_spec=grid_spec,
        out_shape=jax.ShapeDtypeStruct((bd, DSA_HEADS, DSA_HEAD_DIM), BF16),
        compiler_params=_params(("arbitrary", "arbitrary")),
        name="decode_dsa",
    )(page_table, *([cache_k] * g_pages), *([cache_v] * g_pages),
      dqp, mb.reshape(nc, bd, 1, gw), mbn.reshape(bd, 1, LANES), bias_s, dk_new, dv_new)


def _sample_out_body(ol_ref, w_ref, o_ref):
    o_ref[...] = _dot(ol_ref[...], w_ref[...]).astype(BF16)


def _sample_out(o_lat, w_uv_bd):
    n = o_lat.shape[0]
    return pl.pallas_call(
        _sample_out_body,
        out_shape=jax.ShapeDtypeStruct((n, w_uv_bd.shape[1]), BF16),
        compiler_params=pltpu.CompilerParams(vmem_limit_bytes=VMEM_LIMIT_BYTES),
        name="sample_out",
    )(o_lat, w_uv_bd)


DSA_HEAD_ORDER = tuple(h + g * DSA_GROUP for h in range(DSA_GROUP) for g in range(DSA_KV_HEADS))


def _layer_weights(l, ffn_w_gate, ffn_w_up, ffn_w_down, w_in, mla_q_norm, mla_w_uq, mla_kv_norm,
                   mla_w_uk, mla_w_uv, w_out):
    d = w_in.shape[1]
    z = lambda n: jnp.zeros((d, n), F32)
    cq, ckv, kr, dq, dk, dv, iq, ik, iw = jnp.split(w_in[l], np.cumsum(IN_SPLITS)[:-1], axis=1)
    half = MLA_ROPE // 2
    kr_rot = jnp.concatenate([-kr[:, half:], kr[:, :half]], axis=1)
    pad_rope = lambda a: jnp.concatenate([z(ROPE_LANE0), a, z(LANES - ROPE_LANE0 - MLA_ROPE)], axis=1)
    w_in_p = jnp.concatenate([cq, ckv, pad_rope(kr), pad_rope(kr_rot), dq, dk, dv, iq, ik, ik,
                              iw, z(LANES - IDX_HEADS)], axis=1).astype(BF16)
    assert w_in_p.shape[1] == P_COLS

    uq = mla_w_uq[l].reshape(Q_LORA, MLA_HEADS, MLA_NOPE + MLA_ROPE)
    nope, r1, r2 = uq[..., :MLA_NOPE], uq[..., MLA_NOPE:MLA_NOPE + half], uq[..., MLA_NOPE + half:]
    zq = lambda n: jnp.zeros((Q_LORA, MLA_HEADS, n), F32)
    tail = HEAD_TILE - MLA_NOPE - MLA_ROPE
    w_a = jnp.concatenate([nope, r1, r2, zq(tail)], axis=-1).reshape(Q_LORA, -1)
    w_b = jnp.concatenate([zq(MLA_NOPE), -r2, r1, zq(tail)], axis=-1).reshape(Q_LORA, -1)
    w_ab = jnp.concatenate([w_a, w_b], axis=1).astype(BF16)

    uk = mla_w_uk[l]
    w_ukp = jnp.concatenate([uk, jnp.zeros((KV_LORA, MLA_HEADS, HEAD_TILE - MLA_NOPE), F32)],
                            axis=-1).reshape(KV_LORA, -1).astype(BF16)
    w_ukt = jnp.concatenate([jnp.transpose(uk, (1, 2, 0)),
                             jnp.zeros((MLA_HEADS, HEAD_TILE - MLA_NOPE, KV_LORA), F32)], axis=1).astype(BF16)
    uv = mla_w_uv[l]
    w_uv = uv.reshape(KV_LORA, -1).astype(BF16)
    eye = jnp.eye(MLA_HEADS, dtype=F32)
    w_uv_bd = (jnp.transpose(uv, (1, 0, 2))[:, :, None, :] * eye[:, None, :, None]).reshape(
        MLA_HEADS * KV_LORA, MLA_HEADS * MLA_V).astype(BF16)

    mla_w = MLA_HEADS * MLA_V
    wo = w_out[l]
    wo_dsa = wo[mla_w:].reshape(DSA_HEADS, DSA_HEAD_DIM, -1)[jnp.asarray(DSA_HEAD_ORDER)]
    wo_p = jnp.concatenate([wo[:mla_w], wo_dsa.reshape(DSA_HEADS * DSA_HEAD_DIM, -1)], axis=0).astype(BF16)

    nch = D_FF // FF_CHUNK
    chunk_cols = lambda w: jnp.transpose(w.reshape(d, nch, FF_CHUNK), (1, 0, 2)).astype(BF16)
    ffn = [(chunk_cols(ffn_w_gate[l, j]), chunk_cols(ffn_w_up[l, j]),
            ffn_w_down[l, j].reshape(nch, FF_CHUNK, d).astype(BF16)) for j in range(2)]
    return dict(w_in=w_in_p, q_norm=mla_q_norm[l].reshape(1, -1), w_ab=w_ab,
                kv_norm=mla_kv_norm[l].reshape(1, -1), w_ukp=w_ukp, w_ukt=w_ukt, w_uv=w_uv,
                w_uv_bd=w_uv_bd, w_out=wo_p, ffn=ffn)


def _rope_tables(pos):
    half = MLA_ROPE // 2
    inv = ROPE_THETA ** (-jnp.arange(half, dtype=F32) / half)
    ang = pos.astype(F32)[:, None] * inv[None, :]
    n = pos.shape[0]
    tail = jnp.zeros((n, LANES - ROPE_LANE0 - MLA_ROPE), F32)
    cos_t = jnp.concatenate([jnp.ones((n, ROPE_LANE0), F32), jnp.cos(ang), jnp.cos(ang), tail], axis=1)
    sin_t = jnp.concatenate([jnp.zeros((n, ROPE_LANE0), F32), jnp.sin(ang), jnp.sin(ang), tail], axis=1)
    return cos_t, sin_t


def kernel(x_prompt, x_sample, cache_ckv, cache_krope, cache_k, cache_v, cache_idx_k, page_table,
           c_prompt, c_sample, norm_g, w_ada, b_ada, ffn_w_gate, ffn_w_up, ffn_w_down, w_in,
           mla_q_norm, mla_w_uq, mla_kv_norm, mla_w_uk, mla_w_uv, w_out, rel_bias, final_norm):
    b, t, d = x_prompt.shape
    bd, dec_seq, _ = x_sample.shape
    depth = w_in.shape[0]
    n_pages = page_table.shape[1]
    page = cache_ckv.shape[2]
    past = n_pages * page
    assert dec_seq == 1 and page == LANES and d == D_MODEL
    assert t % min(ATT_TILE, t) == 0 and n_pages % min(PAGES_PER_STEP, n_pages) == 0
    k_prompt = min(TOPK_MAX, t // 4)
    k_sample = min(TOPK_MAX, (past + dec_seq) // 4)

    mod = _adaln(jnp.concatenate([c_prompt, c_sample], axis=0), w_ada, b_ada)
    cos_p, sin_p = _rope_tables(jnp.arange(t, dtype=jnp.int32))
    cos_s, sin_s = _rope_tables(jnp.full((bd,), past, dtype=jnp.int32))
    tile = min(ATT_TILE, t)
    bias_p, bias_s = _bias_tables(rel_bias, tile, tile)
    bias_s = jnp.transpose(bias_s, (1, 0, 2))
    slot_minor = lambda a: jnp.swapaxes(a, 2, 3)
    kv_slot_minor = lambda a: jnp.transpose(a, (0, 1, 3, 4, 2)).reshape(a.shape[:2] + (LANES, page))
    krope_t, idx_k_t = slot_minor(cache_krope), slot_minor(cache_idx_k)
    ck, cv = kv_slot_minor(cache_k), kv_slot_minor(cache_v)
    head_order = jnp.asarray(DSA_HEAD_ORDER)

    hp = x_prompt
    hs = x_sample.reshape(1, bd, d)
    rows_p, rows_s = [], []
    for l in range(depth):
        lw = _layer_weights(l, ffn_w_gate, ffn_w_up, ffn_w_down, w_in, mla_q_norm, mla_w_uq,
                            mla_kv_norm, mla_w_uk, mla_w_uv, w_out)
        mod_p = mod[l, :b].reshape(b, 1, -1)
        mod_s = mod[l, b:].reshape(1, bd, -1)
        last = l == depth - 1

        hp = _ffn(hp, mod_p, 0, norm_g[l, 0], *lw["ffn"][0])
        pr = _proj(hp, mod_p, norm_g[l, 1], lw, cos_p, sin_p)
        a_p = _mla_prompt(pr["q"], pr["k"], pr["v"])
        b_p = _dsa_prompt(pr, bias_p, k_prompt)
        hp = _ffn(hp, mod_p, 6, norm_g[l, 2], *lw["ffn"][1], mix=(a_p, b_p, lw["w_out"], 5),
                  final_g=final_norm if last else None)
        rows_p.append((pr["ckv"], pr["kr"], pr["dk"], pr["dv"], pr["ik"]))

        hs = _ffn(hs, mod_s, 0, norm_g[l, 0], *lw["ffn"][0])
        sr = {name: val[0] for name, val in _proj(hs, mod_s, norm_g[l, 1], lw, cos_s, sin_s).items()}
        ql, qr = _sample_q(sr["q"], lw["w_ukt"])
        o_lat, scores = _decode1(
            l, page_table, cache_ckv, krope_t, idx_k_t,
            ql.reshape(bd, MLA_HEADS, KV_LORA), qr.reshape(bd, MLA_HEADS, MLA_ROPE),
            sr["ckv"].reshape(bd, 1, KV_LORA), sr["kr"].reshape(bd, 1, MLA_ROPE),
            sr["iq"].reshape(bd, IDX_HEADS, IDX_DIM), sr["iw"][:, :IDX_HEADS].reshape(bd, IDX_HEADS, 1))
        mb, mbn = _sample_topk(scores, sr["iq"], sr["ikd"], sr["iw"], k_sample)
        o_dsa = _decode2(l, page_table, ck, cv, sr["dqp"].reshape(bd, DSA_HEADS, LANES), mb, mbn, bias_s,
                         sr["dk"].reshape(bd, 1, LANES), sr["dv"].reshape(bd, 1, LANES))
        a_s = _sample_out(o_lat.reshape(bd, MLA_HEADS * KV_LORA), lw["w_uv_bd"])
        b_s = o_dsa[:, head_order].reshape(bd, DSA_HEADS * DSA_HEAD_DIM)
        hs = _ffn(hs, mod_s, 6, norm_g[l, 2], *lw["ffn"][1],
                  mix=(a_s.reshape(1, bd, -1), b_s.reshape(1, bd, -1), lw["w_out"], 5),
                  final_g=final_norm if last else None)
        rows_s.append((sr["ckv"], sr["kr"], sr["dk"], sr["dv"], sr["ik"]))

    p_ckv, p_kr, p_k, p_v, p_ik = [jnp.stack(x, axis=0) for x in zip(*rows_p)]
    s_ckv, s_kr, s_k, s_v, s_ik = [jnp.stack(x, axis=0) for x in zip(*rows_s)]
    kv5 = lambda a, n: a.reshape(depth, n[0], n[1], DSA_KV_HEADS, DSA_HEAD_DIM)
    seq4 = lambda a: a.reshape(depth, bd, dec_seq, a.shape[-1])
    return (hp, hs.reshape(bd, dec_seq, d), p_ckv, p_kr, kv5(p_k, (b, t)), kv5(p_v, (b, t)), p_ik,
            seq4(s_ckv), seq4(s_kr), kv5(s_k, (bd, dec_seq)), kv5(s_v, (bd, dec_seq)), seq4(s_ik))
```

```python
import functools
import math

import numpy as np
import jax
import jax.numpy as jnp
from jax import lax
from jax.experimental import pallas as pl
from jax.experimental.pallas import tpu as pltpu

D_MODEL = 1024
MLA_HEADS = 8
MLA_NOPE = 64
MLA_ROPE = 32
MLA_V = 64
Q_LORA = 384
KV_LORA = 256
DSA_HEADS = 8
DSA_KV_HEADS = 2
DSA_HEAD_DIM = 64
DSA_GROUP = DSA_HEADS // DSA_KV_HEADS
IDX_HEADS = 8
IDX_DIM = 64
TOPK_MAX = 256
N_BUCKETS = 32
MAX_DISTANCE = 128
D_FF = 2816
N_MOD = 9
ROPE_THETA = 10000.0
EPS = 1e-6
MLA_SCALE = (MLA_NOPE + MLA_ROPE) ** -0.5
DSA_SCALE = DSA_HEAD_DIM ** -0.5
IDX_SCALE = IDX_DIM ** -0.5
IN_SPLITS = (Q_LORA, KV_LORA, MLA_ROPE, DSA_HEADS * DSA_HEAD_DIM, DSA_KV_HEADS * DSA_HEAD_DIM,
             DSA_KV_HEADS * DSA_HEAD_DIM, IDX_HEADS * IDX_DIM, IDX_DIM, IDX_HEADS)

LANES = 128
MXU_DIM = 256
VMEM_LIMIT_BYTES = 56 * 1024 * 1024

LOG2E = math.log2(math.e)
NEG = -1e30
F32_LOWEST = float(np.finfo(np.float32).min)
BF16 = jnp.bfloat16
F32 = jnp.float32

FF_CHUNK = MXU_DIM
TOKEN_TILE = 512
ATT_TILE = 256
PAGES_PER_STEP = 16


def _dot(a, b):
    return jnp.dot(a, b, preferred_element_type=F32)


def _dot_nt(a, b):
    return lax.dot_general(a, b, (((1,), (1,)), ((), ())), preferred_element_type=F32)


def _rms(x, g):
    return x * lax.rsqrt(jnp.mean(x * x, axis=-1, keepdims=True) + EPS) * g


def _params(semantics):
    return pltpu.CompilerParams(dimension_semantics=semantics, vmem_limit_bytes=VMEM_LIMIT_BYTES)


def _resident(shape):
    zeros = (0,) * len(shape)
    return pl.BlockSpec(shape, lambda *_: zeros, pipeline_mode=pl.Buffered(1))


def _adaln_body(c_ref, w_ref, b_ref, o_ref):
    o_ref[...] = _dot(c_ref[...].astype(BF16), w_ref[...].astype(BF16)) + b_ref[...]


def _adaln(c_all, w_ada, b_ada):
    depth, d, n = w_ada.shape
    r = c_all.shape[0]
    tn = 1152
    return pl.pallas_call(
        _adaln_body,
        grid=(depth, n // tn),
        in_specs=[pl.BlockSpec((r, d), lambda l, j: (0, 0)),
                  pl.BlockSpec((None, d, tn), lambda l, j: (l, 0, j)),
                  pl.BlockSpec((None, 1, tn), lambda l, j: (l, 0, j))],
        out_specs=pl.BlockSpec((None, r, tn), lambda l, j: (l, 0, j)),
        out_shape=jax.ShapeDtypeStruct((depth, r, n), F32),
        compiler_params=_params(("arbitrary", "arbitrary")),
        name="adaln",
    )(c_all, w_ada, b_ada.reshape(depth, 1, n))


def _ffn_body(*refs, with_mix, with_final):
    x_ref, sh_ref, sc_ref, g_ref, ng_ref, wg_ref, wu_ref, wd_ref = refs[:8]
    rest = list(refs[8:])
    if with_mix:
        a_ref, b_ref, wo_ref, g2_ref = rest[:4]
        rest = rest[4:]
    if with_final:
        fg_ref = rest[0]
        rest = rest[1:]
    o_ref = rest[0]

    x = x_ref[...]
    if with_mix:
        half = a_ref.shape[-1]
        mix = _dot(a_ref[...], wo_ref[:half, :]) + _dot(b_ref[...], wo_ref[half:, :])
        x = x + g2_ref[...] * mix
    h = (_rms(x, ng_ref[...]) * (1.0 + sc_ref[...]) + sh_ref[...]).astype(BF16)
    acc = jnp.zeros(x.shape, F32)
    for c in range(wg_ref.shape[0]):
        gate = _dot(h, wg_ref[c])
        up = _dot(h, wu_ref[c])
        act = (gate * (1.0 / (1.0 + jnp.exp(-gate))) * up).astype(BF16)
        acc = acc + _dot(act, wd_ref[c])
    y = x + 0.5 * g_ref[...] * acc
    if with_final:
        y = _rms(y, fg_ref[...])
    o_ref[...] = y


def _ffn(x, mod, chunk0, norm_g, wg, wu, wd, mix=None, final_g=None):
    b, t, d = x.shape
    r = mod.shape[1]
    tm = min(TOKEN_TILE, t)
    rm = 1 if r == 1 else tm

    def mod_spec(k):
        if r == 1:
            return pl.BlockSpec((None, 1, d), lambda bi, i: (bi, 0, k))
        return pl.BlockSpec((None, rm, d), lambda bi, i: (bi, i, k))

    tok = lambda w: pl.BlockSpec((None, tm, w), lambda bi, i: (bi, i, 0))
    in_specs = [tok(d), mod_spec(chunk0), mod_spec(chunk0 + 1), mod_spec(chunk0 + 2),
                _resident((1, d)), _resident(wg.shape), _resident(wu.shape), _resident(wd.shape)]
    args = [x, mod, mod, mod, norm_g.reshape(1, d), wg, wu, wd]
    if mix is not None:
        a, bb, wo, g2_chunk = mix
        in_specs += [tok(a.shape[-1]), tok(bb.shape[-1]), _resident(wo.shape), mod_spec(g2_chunk)]
        args += [a, bb, wo, mod]
    if final_g is not None:
        in_specs.append(_resident((1, d)))
        args.append(final_g.reshape(1, d))
    return pl.pallas_call(
        functools.partial(_ffn_body, with_mix=mix is not None, with_final=final_g is not None),
        grid=(b, t // tm),
        in_specs=in_specs,
        out_specs=tok(d),
        out_shape=jax.ShapeDtypeStruct((b, t, d), F32),
        compiler_params=_params(("arbitrary", "arbitrary")),
        name="ffn",
    )(*args)


P_CQ = 0
P_CKV = P_CQ + Q_LORA
P_KR = P_CKV + KV_LORA
P_KRR = P_KR + LANES
P_DQ = P_KRR + LANES
P_DK = P_DQ + DSA_HEADS * DSA_HEAD_DIM
P_DV = P_DK + LANES
P_IQ = P_DV + LANES
P_IK = P_IQ + IDX_HEADS * IDX_DIM
P_IW = P_IK + LANES
P_COLS = P_IW + LANES
HEAD_TILE = LANES
ROPE_LANE0 = MLA_NOPE


def _proj_body(x_ref, sh_ref, sc_ref, ng_ref, w_ref, qn_ref, wab_ref, kvn_ref, wuk_ref, wuv_ref,
               cos_ref, sin_ref,
               q_ref, k_ref, v_ref, ckv_ref, kr_ref, dqp_ref, dk_ref, dv_ref, dkb_ref, dvb_ref,
               iq_ref, ik_ref, ikd_ref, iw_ref):
    h = (_rms(x_ref[...], ng_ref[...]) * (1.0 + sc_ref[...]) + sh_ref[...]).astype(BF16)
    cos = cos_ref[...]
    sin = sin_ref[...]
    nq = MLA_HEADS * HEAD_TILE

    cq = _rms(_dot(h, w_ref[:, P_CQ:P_CKV]), qn_ref[...]).astype(BF16)
    ab = _dot(cq, wab_ref[...])
    for hh in range(MLA_HEADS):
        lo = hh * HEAD_TILE
        q_ref[:, lo:lo + HEAD_TILE] = (ab[:, lo:lo + HEAD_TILE] * cos
                                       + ab[:, nq + lo:nq + lo + HEAD_TILE] * sin).astype(BF16)

    ckv = _rms(_dot(h, w_ref[:, P_CKV:P_KR]), kvn_ref[...])
    ckv_ref[...] = ckv
    ckv_b = ckv.astype(BF16)
    kr2 = _dot(h, w_ref[:, P_KR:P_DQ])
    krt = kr2[:, :LANES] * cos + kr2[:, LANES:] * sin
    kr_ref[...] = pltpu.roll(krt, LANES - ROPE_LANE0, axis=1)[:, :MLA_ROPE]
    kn = _dot(ckv_b, wuk_ref[...])
    for hh in range(MLA_HEADS):
        lo = hh * HEAD_TILE
        k_ref[:, lo:lo + HEAD_TILE] = (kn[:, lo:lo + HEAD_TILE] + krt).astype(BF16)
    v_ref[...] = _dot(ckv_b, wuv_ref[...]).astype(BF16)

    dq = _dot(h, w_ref[:, P_DQ:P_DK]) * DSA_SCALE
    lane = lax.broadcasted_iota(jnp.int32, (dq.shape[0], LANES), 1)
    low = lane < DSA_HEAD_DIM
    for j in range(DSA_HEADS // 2):
        t = dq[:, j * LANES:(j + 1) * LANES]
        tr = pltpu.roll(t, DSA_HEAD_DIM, axis=1)
        g = (2 * j) // DSA_GROUP
        even = jnp.where(low, t, 0.0) if g == 0 else jnp.where(low, 0.0, tr)
        odd = jnp.where(low, tr, 0.0) if g == 0 else jnp.where(low, 0.0, t)
        dqp_ref[:, (2 * j) * LANES:(2 * j + 1) * LANES] = even.astype(BF16)
        dqp_ref[:, (2 * j + 1) * LANES:(2 * j + 2) * LANES] = odd.astype(BF16)

    dkv = _dot(h, w_ref[:, P_DK:P_IQ])
    dk_ref[...] = dkv[:, :LANES]
    dv_ref[...] = dkv[:, LANES:]
    dkb_ref[...] = dkv[:, :LANES].astype(BF16)
    dvb_ref[...] = dkv[:, LANES:].astype(BF16)

    iq_ref[...] = _dot(h, w_ref[:, P_IQ:P_IK]).astype(BF16)
    ik2 = _dot(h, w_ref[:, P_IK:P_COLS])
    ik_ref[...] = ik2[:, :IDX_DIM]
    ikd_ref[...] = ik2[:, :LANES].astype(BF16)
    iw_ref[...] = ik2[:, LANES:] * (IDX_HEADS ** -0.5) * IDX_SCALE


def _proj(x, mod, norm_g, lw, cos_t, sin_t):
    b, t, d = x.shape
    r = mod.shape[1]
    tm = min(TOKEN_TILE, t)

    def mod_spec(k):
        if r == 1:
            return pl.BlockSpec((None, 1, d), lambda bi, i: (bi, 0, k))
        return pl.BlockSpec((None, tm, d), lambda bi, i: (bi, i, k))

    tok = lambda w: pl.BlockSpec((None, tm, w), lambda bi, i: (bi, i, 0))
    tab = pl.BlockSpec((tm, LANES), lambda bi, i: (i, 0))
    in_specs = [tok(d), mod_spec(3), mod_spec(4), _resident((1, d)), _resident(lw["w_in"].shape),
                _resident((1, Q_LORA)), _resident(lw["w_ab"].shape), _resident((1, KV_LORA)),
                _resident(lw["w_ukp"].shape), _resident(lw["w_uv"].shape), tab, tab]
    nq = MLA_HEADS * HEAD_TILE
    outs = [("q", nq, BF16), ("k", nq, BF16), ("v", MLA_HEADS * MLA_V, BF16), ("ckv", KV_LORA, F32),
            ("kr", MLA_ROPE, F32), ("dqp", DSA_HEADS * LANES, BF16), ("dk", LANES, F32), ("dv", LANES, F32),
            ("dkb", LANES, BF16), ("dvb", LANES, BF16), ("iq", IDX_HEADS * IDX_DIM, BF16),
            ("ik", IDX_DIM, F32), ("ikd", LANES, BF16), ("iw", LANES, F32)]
    res = pl.pallas_call(
        _proj_body,
        grid=(b, t // tm),
        in_specs=in_specs,
        out_specs=[tok(w) for _, w, _ in outs],
        out_shape=[jax.ShapeDtypeStruct((b, t, w), dt) for _, w, dt in outs],
        compiler_params=_params(("arbitrary", "arbitrary")),
        name="proj",
    )(x, mod, mod, norm_g.reshape(1, d), lw["w_in"], lw["q_norm"], lw["w_ab"], lw["kv_norm"],
      lw["w_ukp"], lw["w_uv"], cos_t, sin_t)
    return {name: val for (name, _, _), val in zip(outs, res)}


def _bucket_table(max_d):
    n = np.arange(max_d + 1, dtype=np.int32)
    max_exact = N_BUCKETS // 2
    nf = np.maximum(n, 1).astype(np.float32)
    large = max_exact + (np.log(nf / np.float32(max_exact)) / np.float32(math.log(MAX_DISTANCE / max_exact))
                         * np.float32(N_BUCKETS - max_exact)).astype(np.int32)
    large = np.minimum(large, N_BUCKETS - 1)
    return np.where(n < max_exact, n, large).astype(np.int32)


def _bias_body(rb_ref, bkt_ref, bkts_ref, bp_ref, bs_ref):
    h = pl.program_id(0)
    bkt = bkt_ref[...]
    bkts = bkts_ref[...]
    tile = jnp.zeros(bkt.shape, F32)
    row = jnp.zeros(bkts.shape, F32)
    for bi in range(N_BUCKETS):
        val = rb_ref[bi, h] * LOG2E
        tile = jnp.where(bkt == bi, val, tile)
        row = jnp.where(bkts == bi, val, row)
    bp_ref[...] = tile
    bs_ref[...] = row


def _bias_tables(rel_bias, tq, tk):
    table = _bucket_table(tq + tk)
    r = np.arange(tq)[:, None]
    c = np.arange(tk)[None, :]
    bkt_p = np.stack([table[np.clip(r - c, 0, None)], table[tq + r - c]]).astype(np.int32)
    d_last = LANES - np.arange(LANES)
    bkt_s = np.stack([table[np.minimum(d_last, tq + tk)], np.full(LANES, N_BUCKETS - 1),
                      np.zeros(LANES, np.int64)]).astype(np.int32)
    return pl.pallas_call(
        _bias_body,
        grid=(DSA_HEADS,),
        in_specs=[pl.BlockSpec(memory_space=pltpu.SMEM),
                  pl.BlockSpec(bkt_p.shape, lambda h: (0, 0, 0)),
                  pl.BlockSpec(bkt_s.shape, lambda h: (0, 0))],
        out_specs=[pl.BlockSpec((None,) + bkt_p.shape, lambda h: (h, 0, 0, 0)),
                   pl.BlockSpec((None,) + bkt_s.shape, lambda h: (h, 0, 0))],
        out_shape=[jax.ShapeDtypeStruct((DSA_HEADS,) + bkt_p.shape, F32),
                   jax.ShapeDtypeStruct((DSA_HEADS,) + bkt_s.shape, F32)],
        compiler_params=_params(("arbitrary",)),
        name="bias_tables",
    )(rel_bias, jnp.asarray(bkt_p), jnp.asarray(bkt_s))


def _lane_tiles(x):
    return [x[:, j * LANES:(j + 1) * LANES] for j in range(x.shape[1] // LANES)]


def _head_step(t, vt, m_ref, l_ref, head, shift=None):
    tiles = _lane_tiles(t)
    fold = tiles[0]
    for x in tiles[1:]:
        fold = jnp.maximum(fold, x)
    rmax = jnp.max(fold, axis=-1, keepdims=True)
    m_old = m_ref[head]
    if shift is None:
        m_new = jnp.maximum(m_old, rmax)
        ref = m_new
    else:
        m_new = jnp.maximum(m_old, rmax + shift)
        ref = m_new - shift
    alpha = jnp.exp2(m_old - m_new)
    ps = [jnp.exp2(x - ref) for x in tiles]
    psum = ps[0]
    for x in ps[1:]:
        psum = psum + x
    l_ref[head] = alpha * l_ref[head] + psum
    m_ref[head] = m_new
    p = jnp.concatenate([x.astype(BF16) for x in ps], axis=1)
    return alpha, _dot(p, vt)


def _pair_update(acc_ref, pair, res0, res1, low):
    acc_ref[pair] = (acc_ref[pair] * jnp.where(low, res0[0], res1[0]) + jnp.where(low, res0[1], res1[1]))


def _softmax_init(m_ref, l_ref, acc_ref):
    m_ref[...] = jnp.full(m_ref.shape, -jnp.inf, F32)
    l_ref[...] = jnp.zeros(l_ref.shape, F32)
    acc_ref[...] = jnp.zeros(acc_ref.shape, F32)


def _pair_out(l_ref, acc_ref, pair, h0, h1, low):
    inv0 = 1.0 / jnp.sum(l_ref[h0], axis=-1, keepdims=True)
    inv1 = 1.0 / jnp.sum(l_ref[h1], axis=-1, keepdims=True)
    return (acc_ref[pair] * jnp.where(low, inv0, inv1)).astype(BF16)


def _mla_prompt_body(q_ref, k_ref, v_ref, o_ref, m_ref, l_ref, acc_ref, *, tile):
    i = pl.program_id(1)
    low = lax.broadcasted_iota(jnp.int32, (tile, LANES), 1) < MLA_V
    row = lax.broadcasted_iota(jnp.int32, (tile, tile), 0)
    col = lax.broadcasted_iota(jnp.int32, (tile, tile), 1)
    scale = MLA_SCALE * LOG2E
    _softmax_init(m_ref, l_ref, acc_ref)

    def step(start, width, causal):
        ks = pl.ds(pl.multiple_of(start, tile), width)
        for pair in range(MLA_HEADS // 2):
            res = []
            for head in (2 * pair, 2 * pair + 1):
                lo = head * HEAD_TILE
                t = _dot_nt(q_ref[:, lo:lo + HEAD_TILE], k_ref[ks, lo:lo + HEAD_TILE]) * scale
                if causal:
                    t = jnp.where(col <= row, t, -jnp.inf)
                res.append(_head_step(t, v_ref[ks, pair * LANES:(pair + 1) * LANES], m_ref, l_ref, head))
            _pair_update(acc_ref, pair, res[0], res[1], low)

    def wide(j, carry):
        step(j * (2 * tile), 2 * tile, False)
        return carry

    lax.fori_loop(0, i // 2, wide, 0)

    @pl.when(i % 2 == 1)
    def _():
        step((i - 1) * tile, tile, False)

    step(i * tile, tile, True)
    for pair in range(MLA_HEADS // 2):
        o_ref[:, pair * LANES:(pair + 1) * LANES] = _pair_out(l_ref, acc_ref, pair, 2 * pair, 2 * pair + 1, low)


def _mla_prompt(q, k, v):
    b, t, nq = q.shape
    tile = min(ATT_TILE, t)
    nv = v.shape[-1]
    return pl.pallas_call(
        functools.partial(_mla_prompt_body, tile=tile),
        grid=(b, t // tile),
        in_specs=[pl.BlockSpec((None, tile, nq), lambda bi, i: (bi, i, 0)),
                  pl.BlockSpec((None, t, nq), lambda bi, i: (bi, 0, 0)),
                  pl.BlockSpec((None, t, nv), lambda bi, i: (bi, 0, 0))],
        out_specs=pl.BlockSpec((None, tile, nv), lambda bi, i: (bi, i, 0)),
        out_shape=jax.ShapeDtypeStruct((b, t, nv), BF16),
        scratch_shapes=[pltpu.VMEM((MLA_HEADS, tile, LANES), F32), pltpu.VMEM((MLA_HEADS, tile, LANES), F32),
                        pltpu.VMEM((MLA_HEADS // 2, tile, LANES), F32)],
        compiler_params=_params(("arbitrary", "arbitrary")),
        name="mla_prompt",
    )(q, k, v)


INT_MIN = -2 ** 31


def _key_to_f32(key):
    bits = jnp.where(key < 0, key ^ jnp.int32(0x7FFFFFFF), key)
    return lax.bitcast_convert_type(bits, F32)


def _kth_largest(count_ge, k, rows, settled):
    cnt0 = count_ge(jnp.zeros((rows, 1), F32))
    pos = cnt0 >= k
    cur0 = jnp.where(pos, jnp.int32(0), jnp.int32(INT_MIN))
    cnt0 = jnp.where(pos, cnt0, jnp.int32(2 ** 30))

    def cond(state):
        it, _, cnt = state
        open_rows = jnp.where(settled | (cnt == k), 0.0, 1.0)
        return (it < 31) & (jnp.max(open_rows) > 0.5)

    def body(state):
        it, cur, cnt = state
        cand = cur + lax.shift_left(jnp.int32(1), jnp.int32(30) - it)
        c = count_ge(_key_to_f32(cand))
        ok = c >= k
        return it + 1, jnp.where(ok, cand, cur), jnp.where(ok, c, cnt)

    _, cur, cnt = lax.while_loop(cond, body, (jnp.int32(0), cur0, cnt0))
    return _key_to_f32(cur), cnt


def _tie_cut(count_eq_below, need, n_cols, rows):
    bits = max(1, int(math.ceil(math.log2(n_cols))))

    def body(it, cur):
        cand = cur + lax.shift_left(jnp.int32(1), jnp.int32(bits - 1) - it)
        return jnp.where(count_eq_below(cand) < need, cand, cur)

    return lax.fori_loop(0, bits, body, jnp.zeros((rows, 1), jnp.int32))


def _dsa_prompt_body(iq_ref, iw_ref, dqp_ref, ikd_ref, dkb_ref, dvb_ref, bias_ref, far_ref, o_ref,
                     s_ref, iqp_ref, wb_ref, cut_ref, cand_ref, pint_ref, m_ref, l_ref, acc_ref, *, tile, k_sel):
    i = pl.program_id(1)
    lane = lax.broadcasted_iota(jnp.int32, (tile, LANES), 1)
    low = lane < DSA_HEAD_DIM
    row = lax.broadcasted_iota(jnp.int32, (tile, tile), 0)
    col = lax.broadcasted_iota(jnp.int32, (tile, tile), 1)
    reps = tile // LANES

    for h in range(IDX_HEADS):
        t = iq_ref[:, (h // 2) * LANES:(h // 2 + 1) * LANES]
        keep = low if h % 2 == 0 else jnp.logical_not(low)
        iqp_ref[h] = jnp.where(keep, t, jnp.zeros_like(t))
        wb_ref[h] = jnp.broadcast_to(iw_ref[:, h:h + 1], (tile, LANES))

    def score_block(kb, causal):
        ks = pl.ds(pl.multiple_of(kb * tile, tile), tile)
        kt = ikd_ref[ks, :]
        sc = jnp.zeros((tile, tile), F32)
        for h in range(IDX_HEADS):
            w = jnp.concatenate([wb_ref[h]] * reps, axis=1)
            sc = sc + jnp.maximum(_dot_nt(iqp_ref[h], kt), 0.0) * w
        if causal:
            sc = jnp.where(col <= row, sc, -jnp.inf)
        s_ref[kb] = sc

    def score_loop(kb, carry):
        score_block(kb, False)
        return carry

    lax.fori_loop(0, i, score_loop, 0)
    score_block(i, True)
    s_ref[i + 1] = jnp.full((tile, tile), -jnp.inf, F32)

    def lane_fold(x):
        out = x[:, :LANES]
        for r in range(1, reps):
            out = out + x[:, r * LANES:(r + 1) * LANES]
        return out

    n_pairs = i // 2 + 1

    half = tile // 2
    col_h = lax.broadcasted_iota(jnp.int32, (half, tile), 1)

    def masked_count(pred):
        accs = []
        for r0 in range(0, tile, half):
            rows = slice(r0, r0 + half)

            def body(j, acc, rows=rows):
                for kb in (2 * j, 2 * j + 1):
                    acc = acc + lane_fold(jnp.where(pred(s_ref[kb, rows, :], kb, rows), 1.0, 0.0))
                return acc

            accs.append(lax.fori_loop(0, n_pairs, body, jnp.zeros((half, LANES), F32)))
        return jnp.sum(jnp.concatenate(accs, axis=0), axis=-1, keepdims=True).astype(jnp.int32)

    rowval = lambda ref, rows: jnp.concatenate([ref[rows, :]] * reps, axis=1)

    def count_ge(cand):
        cand_ref[...] = jnp.broadcast_to(cand, (tile, LANES))
        return masked_count(lambda s, kb, rows: s >= rowval(cand_ref, rows))

    def count_tied_below(p):
        pint_ref[...] = jnp.broadcast_to(p, (tile, LANES))
        return masked_count(lambda s, kb, rows: (s == rowval(cand_ref, rows))
                            & (kb * tile + col_h < rowval(pint_ref, rows)))

    n_adm = i * tile + lax.broadcasted_iota(jnp.int32, (tile, 1), 0) + 1
    keep_all = n_adm <= k_sel
    thr, n_ge = _kth_largest(count_ge, k_sel, tile, keep_all)
    thr = jnp.where(keep_all, F32_LOWEST, thr)
    tied = jnp.logical_not(keep_all) & (n_ge > k_sel)
    cut_ref[...] = jnp.full(cut_ref.shape, 2 ** 30, jnp.int32)

    @pl.when(jnp.max(jnp.where(tied, 1.0, 0.0)) > 0.5)
    def _():
        cand_ref[...] = jnp.broadcast_to(thr, (tile, LANES))
        need = k_sel - masked_count(lambda s, kb, rows: s > rowval(cand_ref, rows))
        cut_ref[...] = _tie_cut(count_tied_below, need, (s_ref.shape[0] - 1) * tile, tile)

    cut = cut_ref[...]

    def mask_loop(j, carry):
        for kb in (2 * j, 2 * j + 1):
            s = s_ref[kb]
            sel = (s > thr) | ((s == thr) & (kb * tile + col <= cut))
            s_ref[kb] = jnp.where(sel, 0.0, NEG)
        return carry

    lax.fori_loop(0, n_pairs, mask_loop, 0)

    _softmax_init(m_ref, l_ref, acc_ref)

    def step(kb, nblk, kind):
        ks = pl.ds(pl.multiple_of(kb * tile, tile), nblk * tile)
        kt = dkb_ref[ks, :]
        vt = dvb_ref[ks, :]
        mb = jnp.concatenate([s_ref[kb + n] for n in range(nblk)], axis=1)
        for pair in range(DSA_GROUP):
            res = []
            for head in (pair, pair + DSA_GROUP):
                s = _dot_nt(dqp_ref[:, head * LANES:(head + 1) * LANES], kt) * LOG2E + mb
                if kind == "far":
                    res.append(_head_step(s, vt, m_ref, l_ref, head, shift=far_ref[head:head + 1, :]))
                else:
                    res.append(_head_step(s + bias_ref[head, 0 if kind == "diag" else 1], vt, m_ref, l_ref, head))
            _pair_update(acc_ref, pair, res[0], res[1], low)

    n_far = jnp.maximum(i - 1, 0)

    def far_wide(j, carry):
        step(2 * j, 2, "far")
        return carry

    lax.fori_loop(0, n_far // 2, far_wide, 0)

    @pl.when(n_far % 2 == 1)
    def _():
        step(n_far - 1, 1, "far")

    @pl.when(i >= 1)
    def _():
        step(i - 1, 1, "prev")

    step(i, 1, "diag")
    for pair in range(DSA_GROUP):
        o_ref[:, pair * LANES:(pair + 1) * LANES] = _pair_out(l_ref, acc_ref, pair, pair, pair + DSA_GROUP, low)


def _dsa_prompt(pr, bias_p, bias_far, k_sel):
    iq, iw, dqp, ikd, dkb, dvb = pr["iq"], pr["iw"], pr["dqp"], pr["ikd"], pr["dkb"], pr["dvb"]
    b, t, _ = iq.shape
    tile = min(ATT_TILE, t)
    assert bias_p.shape[-1] == tile and tile >= MAX_DISTANCE
    blk = lambda w: pl.BlockSpec((None, tile, w), lambda bi, i: (bi, i, 0))
    seq = lambda w: pl.BlockSpec((None, t, w), lambda bi, i: (bi, 0, 0))
    return pl.pallas_call(
        functools.partial(_dsa_prompt_body, tile=tile, k_sel=k_sel),
        grid=(b, t // tile),
        in_specs=[blk(iq.shape[-1]), blk(LANES), blk(dqp.shape[-1]), seq(LANES), seq(LANES), seq(LANES),
                  _resident(bias_p.shape), _resident(bias_far.shape)],
        out_specs=blk(DSA_HEADS * DSA_HEAD_DIM),
        out_shape=jax.ShapeDtypeStruct((b, t, DSA_HEADS * DSA_HEAD_DIM), BF16),
        scratch_shapes=[pltpu.VMEM((t // tile + 1, tile, tile), F32),
                        pltpu.VMEM((IDX_HEADS, tile, LANES), BF16),
                        pltpu.VMEM((IDX_HEADS, tile, LANES), F32),
                        pltpu.VMEM((tile, 1), jnp.int32),
                        pltpu.VMEM((tile, LANES), F32),
                        pltpu.VMEM((tile, LANES), jnp.int32),
                        pltpu.VMEM((DSA_HEADS, tile, LANES), F32), pltpu.VMEM((DSA_HEADS, tile, LANES), F32),
                        pltpu.VMEM((DSA_GROUP, tile, LANES), F32)],
        compiler_params=_params(("arbitrary", "arbitrary")),
        name="dsa_prompt",
    )(iq, iw, dqp, ikd, dkb, dvb, bias_p, bias_far)


def _sample_q_body(q_ref, wukt_ref, ql_ref, qr_ref):
    for h in range(MLA_HEADS):
        qt = q_ref[:, h * HEAD_TILE:(h + 1) * HEAD_TILE]
        ql_ref[:, h * KV_LORA:(h + 1) * KV_LORA] = _dot(qt, wukt_ref[h]).astype(BF16)
        qr = pltpu.roll(qt.astype(F32), LANES - ROPE_LANE0, axis=1)[:, :MLA_ROPE]
        qr_ref[:, h * MLA_ROPE:(h + 1) * MLA_ROPE] = qr.astype(BF16)


def _sample_q(q, w_ukt):
    n = q.shape[0]
    return pl.pallas_call(
        _sample_q_body,
        out_shape=[jax.ShapeDtypeStruct((n, MLA_HEADS * KV_LORA), BF16),
                   jax.ShapeDtypeStruct((n, MLA_HEADS * MLA_ROPE), BF16)],
        compiler_params=pltpu.CompilerParams(vmem_limit_bytes=VMEM_LIMIT_BYTES),
        name="sample_q",
    )(q, w_ukt)


def _decode1_body(pt_ref, *refs, g_pages, scale):
    del pt_ref
    ckv = refs[:g_pages]
    krp = refs[g_pages:2 * g_pages]
    ikp = refs[2 * g_pages:3 * g_pages]
    ql_ref, qr_ref, cn_ref, kn_ref, iq_ref, iw_ref, o_ref, sc_ref, m_ref, l_ref, acc_ref = refs[3 * g_pages:]
    c = pl.program_id(1)
    nc = pl.num_programs(1)

    @pl.when(c == 0)
    def _():
        m_ref[...] = jnp.full(m_ref.shape, -jnp.inf, F32)
        l_ref[...] = jnp.zeros(l_ref.shape, F32)
        acc_ref[...] = jnp.zeros(acc_ref.shape, F32)

    ql = ql_ref[...].astype(F32)
    qr = qr_ref[...].astype(F32)
    iq = iq_ref[...].astype(F32)
    iw = iw_ref[...]
    logits, scores = [], []
    for g in range(g_pages):
        logits.append(_dot_nt(ql, ckv[g][...]) + _dot(qr, krp[g][...]))
        s = jnp.maximum(_dot(iq, ikp[g][...]), 0.0) * iw
        scores.append(jnp.sum(s, axis=0, keepdims=True))
    sc_ref[...] = jnp.concatenate(scores, axis=1)
    t = jnp.concatenate(logits, axis=1) * scale
    m_old = m_ref[...]
    m_new = jnp.maximum(m_old, jnp.max(t, axis=-1, keepdims=True))
    p = jnp.exp2(t - m_new)
    alpha = jnp.exp2(m_old - m_new)
    l_ref[...] = alpha * l_ref[...] + jnp.sum(p, axis=-1, keepdims=True)
    m_ref[...] = m_new
    pv = jnp.zeros(acc_ref.shape, F32)
    for g in range(g_pages):
        pv = pv + _dot(p[:, g * LANES:(g + 1) * LANES], ckv[g][...])
    acc_ref[...] = acc_ref[...] * alpha + pv

    @pl.when(c == nc - 1)
    def _():
        cn = cn_ref[...].astype(BF16).astype(F32)
        kn = kn_ref[...].astype(BF16).astype(F32)
        ln = (jnp.sum(ql * cn, axis=-1, keepdims=True) + jnp.sum(qr * kn, axis=-1, keepdims=True)) * scale
        m_old = m_ref[...]
        m_new = jnp.maximum(m_old, ln)
        pn = jnp.exp2(ln - m_new)
        alpha = jnp.exp2(m_old - m_new)
        l = alpha * l_ref[...] + pn
        acc = acc_ref[...] * alpha + pn.astype(BF16).astype(F32) * cn
        o_ref[...] = (acc / l).astype(BF16)


def _decode1(layer, page_table, cache_ckv, cache_krope, cache_idx_k, ql, qr, ckv_new, kr_new, iq, iw):
    bd, n_pages = page_table.shape
    g_pages = min(PAGES_PER_STEP, n_pages)
    nc = n_pages // g_pages
    gw = g_pages * LANES

    def page_spec(rows, width, g):
        return pl.BlockSpec((None, None, rows, width),
                            lambda b, c, pt: (layer, pt[b, c * g_pages + g], 0, 0))

    per_seq = lambda r, w: pl.BlockSpec((None, r, w), lambda b, c, pt: (b, 0, 0))
    in_specs = ([page_spec(LANES, KV_LORA, g) for g in range(g_pages)]
                + [page_spec(MLA_ROPE, LANES, g) for g in range(g_pages)]
                + [page_spec(IDX_DIM, LANES, g) for g in range(g_pages)]
                + [per_seq(MLA_HEADS, KV_LORA), per_seq(MLA_HEADS, MLA_ROPE), per_seq(1, KV_LORA),
                   per_seq(1, MLA_ROPE), per_seq(IDX_HEADS, IDX_DIM), per_seq(IDX_HEADS, 1)])
    grid_spec = pltpu.PrefetchScalarGridSpec(
        num_scalar_prefetch=1,
        grid=(bd, nc),
        in_specs=in_specs,
        out_specs=[per_seq(MLA_HEADS, KV_LORA),
                   pl.BlockSpec((None, None, 1, gw), lambda b, c, pt: (c, b, 0, 0))],
        scratch_shapes=[pltpu.VMEM((MLA_HEADS, 1), F32), pltpu.VMEM((MLA_HEADS, 1), F32),
                        pltpu.VMEM((MLA_HEADS, KV_LORA), F32)],
    )
    o_lat, scores = pl.pallas_call(
        functools.partial(_decode1_body, g_pages=g_pages, scale=MLA_SCALE * LOG2E),
        grid_spec=grid_spec,
        out_shape=[jax.ShapeDtypeStruct((bd, MLA_HEADS, KV_LORA), BF16),
                   jax.ShapeDtypeStruct((nc, bd, 1, gw), F32)],
        compiler_params=_params(("arbitrary", "arbitrary")),
        name="decode_mla_idx",
    )(page_table, *([cache_ckv] * g_pages), *([cache_krope] * g_pages), *([cache_idx_k] * g_pages),
      ql, qr, ckv_new, kr_new, iq, iw)
    return o_lat, scores.reshape(nc, bd, gw)


def _sample_topk_body(s_ref, iq_ref, ikd_ref, iw_ref, mb_ref, mbn_ref, *, k_sel):
    nc, rows, gw = s_ref.shape
    col = lax.broadcasted_iota(jnp.int32, (rows, gw), 1)
    reps = gw // LANES

    prod = iq_ref[...].astype(F32) * jnp.concatenate([ikd_ref[...].astype(F32)] * (IDX_HEADS // 2), axis=1)
    s_new = jnp.zeros((rows, 1), F32)
    for h in range(IDX_HEADS):
        dot_h = jnp.sum(prod[:, h * IDX_DIM:(h + 1) * IDX_DIM], axis=-1, keepdims=True)
        s_new = s_new + jnp.maximum(dot_h, 0.0) * iw_ref[:, h:h + 1]

    def lane_fold(x):
        out = x[:, :LANES]
        for r in range(1, reps):
            out = out + x[:, r * LANES:(r + 1) * LANES]
        return out

    def masked_count(pred, pred_new):
        acc = jnp.zeros((rows, LANES), F32)
        for c in range(nc):
            acc = acc + lane_fold(jnp.where(pred(s_ref[c], c), 1.0, 0.0))
        cnt = jnp.sum(acc, axis=-1, keepdims=True) + jnp.where(pred_new, 1.0, 0.0)
        return cnt.astype(jnp.int32)

    n_cols = nc * gw
    thr, _ = _kth_largest(lambda cand: masked_count(lambda s, c: s >= cand, s_new >= cand), k_sel, rows,
                          jnp.zeros((rows, 1), jnp.bool_))
    need = k_sel - masked_count(lambda s, c: s > thr, s_new > thr)
    cut = _tie_cut(lambda p: masked_count(lambda s, c: (s == thr) & (c * gw + col < p),
                                          (s_new == thr) & (n_cols < p)),
                   need, n_cols + 1, rows)
    for c in range(nc):
        s = s_ref[c]
        sel = (s > thr) | ((s == thr) & (c * gw + col <= cut))
        mb_ref[c] = jnp.where(sel, 0.0, NEG)
    sel_new = (s_new > thr) | ((s_new == thr) & (n_cols <= cut))
    mbn_ref[...] = jnp.broadcast_to(jnp.where(sel_new, 0.0, NEG), mbn_ref.shape)


def _sample_topk(scores, iq, ikd, iw, k_sel):
    nc, bd, gw = scores.shape
    rows = min(32, bd)
    return pl.pallas_call(
        functools.partial(_sample_topk_body, k_sel=k_sel),
        grid=(bd // rows,),
        in_specs=[pl.BlockSpec((nc, rows, gw), lambda r: (0, r, 0)),
                  pl.BlockSpec((rows, iq.shape[-1]), lambda r: (r, 0)),
                  pl.BlockSpec((rows, LANES), lambda r: (r, 0)),
                  pl.BlockSpec((rows, LANES), lambda r: (r, 0))],
        out_specs=[pl.BlockSpec((nc, rows, gw), lambda r: (0, r, 0)),
                   pl.BlockSpec((rows, LANES), lambda r: (r, 0))],
        out_shape=[jax.ShapeDtypeStruct((nc, bd, gw), F32), jax.ShapeDtypeStruct((bd, LANES), F32)],
        compiler_params=_params(("arbitrary",)),
        name="sample_topk",
    )(scores, iq, ikd, iw)


def _decode2_body(pt_ref, *refs, g_pages):
    del pt_ref
    kp = refs[:g_pages]
    vp = refs[g_pages:2 * g_pages]
    q_ref, mb_ref, mbn_ref, bias_ref, dkn_ref, dvn_ref, o_ref, m_ref, l_ref, acc_ref = refs[2 * g_pages:]
    c = pl.program_id(1)
    nc = pl.num_programs(1)

    @pl.when(c == 0)
    def _():
        m_ref[...] = jnp.full(m_ref.shape, -jnp.inf, F32)
        l_ref[...] = jnp.zeros(l_ref.shape, F32)
        acc_ref[...] = jnp.zeros(acc_ref.shape, F32)

    q = q_ref[...].astype(F32)
    mb = mb_ref[...]
    far = bias_ref[1]
    last = jnp.where(c == nc - 1, bias_ref[0], far)
    logits = []
    for g in range(g_pages):
        bias = last if g == g_pages - 1 else far
        logits.append(_dot(q, kp[g][...]) * LOG2E + (bias + mb[:, g * LANES:(g + 1) * LANES]))
    t = jnp.concatenate(logits, axis=1)
    m_old = m_ref[...]
    m_new = jnp.maximum(m_old, jnp.max(t, axis=-1, keepdims=True))
    p = jnp.exp2(t - m_new)
    alpha = jnp.exp2(m_old - m_new)
    l_ref[...] = alpha * l_ref[...] + jnp.sum(p, axis=-1, keepdims=True)
    m_ref[...] = m_new
    pv = jnp.zeros(acc_ref.shape, F32)
    for g in range(g_pages):
        pv = pv + _dot_nt(p[:, g * LANES:(g + 1) * LANES], vp[g][...])
    acc_ref[...] = acc_ref[...] * alpha + pv

    @pl.when(c == nc - 1)
    def _():
        dkn = dkn_ref[...].astype(BF16).astype(F32)
        dvn = dvn_ref[...].astype(BF16).astype(F32)
        ln = (jnp.sum(q * dkn, axis=-1, keepdims=True) * LOG2E
              + (bias_ref[2][:, 0:1] + mbn_ref[:, 0:1]))
        m_old = m_ref[...]
        m_new = jnp.maximum(m_old, ln)
        pn = jnp.exp2(ln - m_new)
        alpha = jnp.exp2(m_old - m_new)
        l = alpha * l_ref[...] + pn
        out = (acc_ref[...] * alpha + pn.astype(BF16).astype(F32) * dvn) / l
        rowi = lax.broadcasted_iota(jnp.int32, out.shape, 0)
        out = jnp.where(rowi < DSA_GROUP, out, pltpu.roll(out, DSA_HEAD_DIM, axis=1))
        o_ref[...] = out[:, :DSA_HEAD_DIM].astype(BF16)


def _decode2(layer, page_table, cache_k, cache_v, dqp, mb, mbn, bias_s, dk_new, dv_new):
    bd, n_pages = page_table.shape
    g_pages = min(PAGES_PER_STEP, n_pages)
    nc = n_pages // g_pages
    gw = g_pages * LANES

    def page_spec(g):
        return pl.BlockSpec((None, None, LANES, LANES),
                            lambda b, c, pt: (layer, pt[b, c * g_pages + g], 0, 0))

    per_seq = lambda r, w: pl.BlockSpec((None, r, w), lambda b, c, pt: (b, 0, 0))
    in_specs = ([page_spec(g) for g in range(g_pages)] + [page_spec(g) for g in range(g_pages)]
                + [per_seq(DSA_HEADS, LANES),
                   pl.BlockSpec((None, None, 1, gw), lambda b, c, pt: (c, b, 0, 0)),
                   per_seq(1, LANES),
                   pl.BlockSpec(bias_s.shape, lambda b, c, pt: (0, 0, 0)),
                   per_seq(1, LANES), per_seq(1, LANES)])
    grid_spec = pltpu.PrefetchScalarGridSpec(
        num_scalar_prefetch=1,
        grid=(bd, nc),
        in_specs=in_specs,
        out_specs=per_seq(DSA_HEADS, DSA_HEAD_DIM),
        scratch_shapes=[pltpu.VMEM((DSA_HEADS, 1), F32), pltpu.VMEM((DSA_HEADS, 1), F32),
                        pltpu.VMEM((DSA_HEADS, LANES), F32)],
    )
    return pl.pallas_call(
        functools.partial(_decode2_body, g_pages=g_pages),
        grid_spec=grid_spec,
        out_shape=jax.ShapeDtypeStruct((bd, DSA_HEADS, DSA_HEAD_DIM), BF16),
        compiler_params=_params(("arbitrary", "arbitrary")),
        name="decode_dsa",
    )(page_table, *([cache_k] * g_pages), *([cache_v] * g_pages),
      dqp, mb.reshape(nc, bd, 1, gw), mbn.reshape(bd, 1, LANES), bias_s, dk_new, dv_new)


def _sample_out_body(ol_ref, w_ref, o_ref):
    o_ref[...] = _dot(ol_ref[...], w_ref[...]).astype(BF16)


def _sample_out(o_lat, w_uv_bd):
    n = o_lat.shape[0]
    return pl.pallas_call(
        _sample_out_body,
        out_shape=jax.ShapeDtypeStruct((n, w_uv_bd.shape[1]), BF16),
        compiler_params=pltpu.CompilerParams(vmem_limit_bytes=VMEM_LIMIT_BYTES),
        name="sample_out",
    )(o_lat, w_uv_bd)


DSA_HEAD_ORDER = tuple(h + g * DSA_GROUP for h in range(DSA_GROUP) for g in range(DSA_KV_HEADS))


def _layer_weights(l, ffn_w_gate, ffn_w_up, ffn_w_down, w_in, mla_q_norm, mla_w_uq, mla_kv_norm,
                   mla_w_uk, mla_w_uv, w_out):
    d = w_in.shape[1]
    z = lambda n: jnp.zeros((d, n), F32)
    cq, ckv, kr, dq, dk, dv, iq, ik, iw = jnp.split(w_in[l], np.cumsum(IN_SPLITS)[:-1], axis=1)
    half = MLA_ROPE // 2
    kr_rot = jnp.concatenate([-kr[:, half:], kr[:, :half]], axis=1)
    pad_rope = lambda a: jnp.concatenate([z(ROPE_LANE0), a, z(LANES - ROPE_LANE0 - MLA_ROPE)], axis=1)
    w_in_p = jnp.concatenate([cq, ckv, pad_rope(kr), pad_rope(kr_rot), dq, dk, dv, iq, ik, ik,
                              iw, z(LANES - IDX_HEADS)], axis=1).astype(BF16)
    assert w_in_p.shape[1] == P_COLS

    uq = mla_w_uq[l].reshape(Q_LORA, MLA_HEADS, MLA_NOPE + MLA_ROPE)
    nope, r1, r2 = uq[..., :MLA_NOPE], uq[..., MLA_NOPE:MLA_NOPE + half], uq[..., MLA_NOPE + half:]
    zq = lambda n: jnp.zeros((Q_LORA, MLA_HEADS, n), F32)
    tail = HEAD_TILE - MLA_NOPE - MLA_ROPE
    w_a = jnp.concatenate([nope, r1, r2, zq(tail)], axis=-1).reshape(Q_LORA, -1)
    w_b = jnp.concatenate([zq(MLA_NOPE), -r2, r1, zq(tail)], axis=-1).reshape(Q_LORA, -1)
    w_ab = jnp.concatenate([w_a, w_b], axis=1).astype(BF16)

    uk = mla_w_uk[l]
    w_ukp = jnp.concatenate([uk, jnp.zeros((KV_LORA, MLA_HEADS, HEAD_TILE - MLA_NOPE), F32)],
                            axis=-1).reshape(KV_LORA, -1).astype(BF16)
    w_ukt = jnp.concatenate([jnp.transpose(uk, (1, 2, 0)),
                             jnp.zeros((MLA_HEADS, HEAD_TILE - MLA_NOPE, KV_LORA), F32)], axis=1).astype(BF16)
    uv = mla_w_uv[l]
    w_uv = uv.reshape(KV_LORA, -1).astype(BF16)
    eye = jnp.eye(MLA_HEADS, dtype=F32)
    w_uv_bd = (jnp.transpose(uv, (1, 0, 2))[:, :, None, :] * eye[:, None, :, None]).reshape(
        MLA_HEADS * KV_LORA, MLA_HEADS * MLA_V).astype(BF16)

    mla_w = MLA_HEADS * MLA_V
    wo = w_out[l]
    wo_dsa = wo[mla_w:].reshape(DSA_HEADS, DSA_HEAD_DIM, -1)[jnp.asarray(DSA_HEAD_ORDER)]
    wo_p = jnp.concatenate([wo[:mla_w], wo_dsa.reshape(DSA_HEADS * DSA_HEAD_DIM, -1)], axis=0).astype(BF16)

    nch = D_FF // FF_CHUNK
    chunk_cols = lambda w: jnp.transpose(w.reshape(d, nch, FF_CHUNK), (1, 0, 2)).astype(BF16)
    ffn = [(chunk_cols(ffn_w_gate[l, j]), chunk_cols(ffn_w_up[l, j]),
            ffn_w_down[l, j].reshape(nch, FF_CHUNK, d).astype(BF16)) for j in range(2)]
    return dict(w_in=w_in_p, q_norm=mla_q_norm[l].reshape(1, -1), w_ab=w_ab,
                kv_norm=mla_kv_norm[l].reshape(1, -1), w_ukp=w_ukp, w_ukt=w_ukt, w_uv=w_uv,
                w_uv_bd=w_uv_bd, w_out=wo_p, ffn=ffn)


def _rope_tables(pos):
    half = MLA_ROPE // 2
    inv = ROPE_THETA ** (-jnp.arange(half, dtype=F32) / half)
    ang = pos.astype(F32)[:, None] * inv[None, :]
    n = pos.shape[0]
    tail = jnp.zeros((n, LANES - ROPE_LANE0 - MLA_ROPE), F32)
    cos_t = jnp.concatenate([jnp.ones((n, ROPE_LANE0), F32), jnp.cos(ang), jnp.cos(ang), tail], axis=1)
    sin_t = jnp.concatenate([jnp.zeros((n, ROPE_LANE0), F32), jnp.sin(ang), jnp.sin(ang), tail], axis=1)
    return cos_t, sin_t


def kernel(x_prompt, x_sample, cache_ckv, cache_krope, cache_k, cache_v, cache_idx_k, page_table,
           c_prompt, c_sample, norm_g, w_ada, b_ada, ffn_w_gate, ffn_w_up, ffn_w_down, w_in,
           mla_q_norm, mla_w_uq, mla_kv_norm, mla_w_uk, mla_w_uv, w_out, rel_bias, final_norm):
    b, t, d = x_prompt.shape
    bd, dec_seq, _ = x_sample.shape
    depth = w_in.shape[0]
    n_pages = page_table.shape[1]
    page = cache_ckv.shape[2]
    past = n_pages * page
    assert dec_seq == 1 and page == LANES and d == D_MODEL
    assert t % min(ATT_TILE, t) == 0 and n_pages % min(PAGES_PER_STEP, n_pages) == 0
    k_prompt = min(TOPK_MAX, t // 4)
    k_sample = min(TOPK_MAX, (past + dec_seq) // 4)

    mod = _adaln(jnp.concatenate([c_prompt, c_sample], axis=0), w_ada, b_ada)
    cos_p, sin_p = _rope_tables(jnp.arange(t, dtype=jnp.int32))
    cos_s, sin_s = _rope_tables(jnp.full((bd,), past, dtype=jnp.int32))
    tile = min(ATT_TILE, t)
    bias_p, bias_s = _bias_tables(rel_bias, tile, tile)
    bias_s = jnp.transpose(bias_s, (1, 0, 2))
    slot_minor = lambda a: jnp.swapaxes(a, 2, 3)
    kv_slot_minor = lambda a: jnp.transpose(a, (0, 1, 3, 4, 2)).reshape(a.shape[:2] + (LANES, page))
    krope_t, idx_k_t = slot_minor(cache_krope), slot_minor(cache_idx_k)
    ck, cv = kv_slot_minor(cache_k), kv_slot_minor(cache_v)
    head_order = jnp.asarray(DSA_HEAD_ORDER)

    hp = x_prompt
    hs = x_sample.reshape(1, bd, d)
    rows_p, rows_s = [], []
    for l in range(depth):
        lw = _layer_weights(l, ffn_w_gate, ffn_w_up, ffn_w_down, w_in, mla_q_norm, mla_w_uq,
                            mla_kv_norm, mla_w_uk, mla_w_uv, w_out)
        mod_p = mod[l, :b].reshape(b, 1, -1)
        mod_s = mod[l, b:].reshape(1, bd, -1)
        last = l == depth - 1

        hp = _ffn(hp, mod_p, 0, norm_g[l, 0], *lw["ffn"][0])
        pr = _proj(hp, mod_p, norm_g[l, 1], lw, cos_p, sin_p)
        a_p = _mla_prompt(pr["q"], pr["k"], pr["v"])
        b_p = _dsa_prompt(pr, bias_p, bias_s[1], k_prompt)
        hp = _ffn(hp, mod_p, 6, norm_g[l, 2], *lw["ffn"][1], mix=(a_p, b_p, lw["w_out"], 5),
                  final_g=final_norm if last else None)
        rows_p.append((pr["ckv"], pr["kr"], pr["dk"], pr["dv"], pr["ik"]))

        hs = _ffn(hs, mod_s, 0, norm_g[l, 0], *lw["ffn"][0])
        sr = {name: val[0] for name, val in _proj(hs, mod_s, norm_g[l, 1], lw, cos_s, sin_s).items()}
        ql, qr = _sample_q(sr["q"], lw["w_ukt"])
        o_lat, scores = _decode1(
            l, page_table, cache_ckv, krope_t, idx_k_t,
            ql.reshape(bd, MLA_HEADS, KV_LORA), qr.reshape(bd, MLA_HEADS, MLA_ROPE),
            sr["ckv"].reshape(bd, 1, KV_LORA), sr["kr"].reshape(bd, 1, MLA_ROPE),
            sr["iq"].reshape(bd, IDX_HEADS, IDX_DIM), sr["iw"][:, :IDX_HEADS].reshape(bd, IDX_HEADS, 1))
        mb, mbn = _sample_topk(scores, sr["iq"], sr["ikd"], sr["iw"], k_sample)
        o_dsa = _decode2(l, page_table, ck, cv, sr["dqp"].reshape(bd, DSA_HEADS, LANES), mb, mbn, bias_s,
                         sr["dk"].reshape(bd, 1, LANES), sr["dv"].reshape(bd, 1, LANES))
        a_s = _sample_out(o_lat.reshape(bd, MLA_HEADS * KV_LORA), lw["w_uv_bd"])
        b_s = o_dsa[:, head_order].reshape(bd, DSA_HEADS * DSA_HEAD_DIM)
        hs = _ffn(hs, mod_s, 6, norm_g[l, 2], *lw["ffn"][1],
                  mix=(a_s.reshape(1, bd, -1), b_s.reshape(1, bd, -1), lw["w_out"], 5),
                  final_g=final_norm if last else None)
        rows_s.append((sr["ckv"], sr["kr"], sr["dk"], sr["dv"], sr["ik"]))

    p_ckv, p_kr, p_k, p_v, p_ik = [jnp.stack(x, axis=0) for x in zip(*rows_p)]
    s_ckv, s_kr, s_k, s_v, s_ik = [jnp.stack(x, axis=0) for x in zip(*rows_s)]
    kv5 = lambda a, n: a.reshape(depth, n[0], n[1], DSA_KV_HEADS, DSA_HEAD_DIM)
    seq4 = lambda a: a.reshape(depth, bd, dec_seq, a.shape[-1])
    return (hp, hs.reshape(bd, dec_seq, d), p_ckv, p_kr, kv5(p_k, (b, t)), kv5(p_v, (b, t)), p_ik,
            seq4(s_ckv), seq4(s_kr), kv5(s_k, (bd, dec_seq)), kv5(s_v, (bd, dec_seq)), seq4(s_ik))
```

```python
import functools
import math

import numpy as np
import jax
import jax.numpy as jnp
from jax import lax
from jax.experimental import pallas as pl
from jax.experimental.pallas import tpu as pltpu

D_MODEL = 1024
MLA_HEADS = 8
MLA_NOPE = 64
MLA_ROPE = 32
MLA_V = 64
Q_LORA = 384
KV_LORA = 256
DSA_HEADS = 8
DSA_KV_HEADS = 2
DSA_HEAD_DIM = 64
DSA_GROUP = DSA_HEADS // DSA_KV_HEADS
IDX_HEADS = 8
IDX_DIM = 64
TOPK_MAX = 256
N_BUCKETS = 32
MAX_DISTANCE = 128
D_FF = 2816
N_MOD = 9
ROPE_THETA = 10000.0
EPS = 1e-6
MLA_SCALE = (MLA_NOPE + MLA_ROPE) ** -0.5
DSA_SCALE = DSA_HEAD_DIM ** -0.5
IDX_SCALE = IDX_DIM ** -0.5
IN_SPLITS = (Q_LORA, KV_LORA, MLA_ROPE, DSA_HEADS * DSA_HEAD_DIM, DSA_KV_HEADS * DSA_HEAD_DIM,
             DSA_KV_HEADS * DSA_HEAD_DIM, IDX_HEADS * IDX_DIM, IDX_DIM, IDX_HEADS)

LANES = 128
MXU_DIM = 256
VMEM_LIMIT_BYTES = 56 * 1024 * 1024

LOG2E = math.log2(math.e)
NEG = -1e30
F32_LOWEST = float(np.finfo(np.float32).min)
BF16 = jnp.bfloat16
F32 = jnp.float32

FF_CHUNK = MXU_DIM
TOKEN_TILE = 512
ATT_TILE = 256
PAGES_PER_STEP = 16


def _dot(a, b):
    return jnp.dot(a, b, preferred_element_type=F32)


def _dot_nt(a, b):
    return lax.dot_general(a, b, (((1,), (1,)), ((), ())), preferred_element_type=F32)


def _rms(x, g):
    return x * lax.rsqrt(jnp.mean(x * x, axis=-1, keepdims=True) + EPS) * g


def _params(semantics):
    return pltpu.CompilerParams(dimension_semantics=semantics, vmem_limit_bytes=VMEM_LIMIT_BYTES)


def _resident(shape):
    zeros = (0,) * len(shape)
    return pl.BlockSpec(shape, lambda *_: zeros, pipeline_mode=pl.Buffered(1))


def _adaln_body(c_ref, w_ref, b_ref, o_ref):
    o_ref[...] = _dot(c_ref[...].astype(BF16), w_ref[...].astype(BF16)) + b_ref[...]


def _adaln(c_all, w_ada, b_ada):
    depth, d, n = w_ada.shape
    r = c_all.shape[0]
    tn = 1152
    return pl.pallas_call(
        _adaln_body,
        grid=(depth, n // tn),
        in_specs=[pl.BlockSpec((r, d), lambda l, j: (0, 0)),
                  pl.BlockSpec((None, d, tn), lambda l, j: (l, 0, j)),
                  pl.BlockSpec((None, 1, tn), lambda l, j: (l, 0, j))],
        out_specs=pl.BlockSpec((None, r, tn), lambda l, j: (l, 0, j)),
        out_shape=jax.ShapeDtypeStruct((depth, r, n), F32),
        compiler_params=_params(("arbitrary", "arbitrary")),
        name="adaln",
    )(c_all, w_ada, b_ada.reshape(depth, 1, n))


def _ffn_body(*refs, with_mix, with_final):
    x_ref, sh_ref, sc_ref, g_ref, ng_ref, wg_ref, wu_ref, wd_ref = refs[:8]
    rest = list(refs[8:])
    if with_mix:
        a_ref, b_ref, wo_ref, g2_ref = rest[:4]
        rest = rest[4:]
    if with_final:
        fg_ref = rest[0]
        rest = rest[1:]
    o_ref = rest[0]

    x = x_ref[...]
    if with_mix:
        half = a_ref.shape[-1]
        mix = _dot(a_ref[...], wo_ref[:half, :]) + _dot(b_ref[...], wo_ref[half:, :])
        x = x + g2_ref[...] * mix
    h = (_rms(x, ng_ref[...]) * (1.0 + sc_ref[...]) + sh_ref[...]).astype(BF16)
    acc = jnp.zeros(x.shape, F32)
    for c in range(wg_ref.shape[0]):
        gate = _dot(h, wg_ref[c])
        up = _dot(h, wu_ref[c])
        act = (gate * (1.0 / (1.0 + jnp.exp(-gate))) * up).astype(BF16)
        acc = acc + _dot(act, wd_ref[c])
    y = x + 0.5 * g_ref[...] * acc
    if with_final:
        y = _rms(y, fg_ref[...])
    o_ref[...] = y


def _ffn(x, mod, chunk0, norm_g, wg, wu, wd, mix=None, final_g=None):
    b, t, d = x.shape
    r = mod.shape[1]
    tm = min(TOKEN_TILE, t)
    rm = 1 if r == 1 else tm

    def mod_spec(k):
        if r == 1:
            return pl.BlockSpec((None, 1, d), lambda bi, i: (bi, 0, k))
        return pl.BlockSpec((None, rm, d), lambda bi, i: (bi, i, k))

    tok = lambda w: pl.BlockSpec((None, tm, w), lambda bi, i: (bi, i, 0))
    in_specs = [tok(d), mod_spec(chunk0), mod_spec(chunk0 + 1), mod_spec(chunk0 + 2),
                _resident((1, d)), _resident(wg.shape), _resident(wu.shape), _resident(wd.shape)]
    args = [x, mod, mod, mod, norm_g.reshape(1, d), wg, wu, wd]
    if mix is not None:
        a, bb, wo, g2_chunk = mix
        in_specs += [tok(a.shape[-1]), tok(bb.shape[-1]), _resident(wo.shape), mod_spec(g2_chunk)]
        args += [a, bb, wo, mod]
    if final_g is not None:
        in_specs.append(_resident((1, d)))
        args.append(final_g.reshape(1, d))
    return pl.pallas_call(
        functools.partial(_ffn_body, with_mix=mix is not None, with_final=final_g is not None),
        grid=(b, t // tm),
        in_specs=in_specs,
        out_specs=tok(d),
        out_shape=jax.ShapeDtypeStruct((b, t, d), F32),
        compiler_params=_params(("arbitrary", "arbitrary")),
        name="ffn",
    )(*args)


P_CQ = 0
P_CKV = P_CQ + Q_LORA
P_KR = P_CKV + KV_LORA
P_KRR = P_KR + LANES
P_DQ = P_KRR + LANES
P_DK = P_DQ + DSA_HEADS * DSA_HEAD_DIM
P_DV = P_DK + LANES
P_IQ = P_DV + LANES
P_IK = P_IQ + IDX_HEADS * IDX_DIM
P_IW = P_IK + LANES
P_COLS = P_IW + LANES
HEAD_TILE = LANES
ROPE_LANE0 = MLA_NOPE


def _proj_body(x_ref, sh_ref, sc_ref, ng_ref, w_ref, qn_ref, wab_ref, kvn_ref, wuk_ref, wuv_ref,
               cos_ref, sin_ref,
               q_ref, k_ref, v_ref, ckv_ref, kr_ref, dqp_ref, dk_ref, dv_ref, dkb_ref, dvb_ref,
               iq_ref, ik_ref, ikd_ref, iw_ref):
    h = (_rms(x_ref[...], ng_ref[...]) * (1.0 + sc_ref[...]) + sh_ref[...]).astype(BF16)
    cos = cos_ref[...]
    sin = sin_ref[...]
    nq = MLA_HEADS * HEAD_TILE

    cq = _rms(_dot(h, w_ref[:, P_CQ:P_CKV]), qn_ref[...]).astype(BF16)
    ab = _dot(cq, wab_ref[...])
    for hh in range(MLA_HEADS):
        lo = hh * HEAD_TILE
        q_ref[:, lo:lo + HEAD_TILE] = (ab[:, lo:lo + HEAD_TILE] * cos
                                       + ab[:, nq + lo:nq + lo + HEAD_TILE] * sin).astype(BF16)

    ckv = _rms(_dot(h, w_ref[:, P_CKV:P_KR]), kvn_ref[...])
    ckv_ref[...] = ckv
    ckv_b = ckv.astype(BF16)
    kr2 = _dot(h, w_ref[:, P_KR:P_DQ])
    krt = kr2[:, :LANES] * cos + kr2[:, LANES:] * sin
    kr_ref[...] = pltpu.roll(krt, LANES - ROPE_LANE0, axis=1)[:, :MLA_ROPE]
    kn = _dot(ckv_b, wuk_ref[...])
    for hh in range(MLA_HEADS):
        lo = hh * HEAD_TILE
        k_ref[:, lo:lo + HEAD_TILE] = (kn[:, lo:lo + HEAD_TILE] + krt).astype(BF16)
    v_ref[...] = _dot(ckv_b, wuv_ref[...]).astype(BF16)

    dq = _dot(h, w_ref[:, P_DQ:P_DK]) * DSA_SCALE
    lane = lax.broadcasted_iota(jnp.int32, (dq.shape[0], LANES), 1)
    low = lane < DSA_HEAD_DIM
    for j in range(DSA_HEADS // 2):
        t = dq[:, j * LANES:(j + 1) * LANES]
        tr = pltpu.roll(t, DSA_HEAD_DIM, axis=1)
        g = (2 * j) // DSA_GROUP
        even = jnp.where(low, t, 0.0) if g == 0 else jnp.where(low, 0.0, tr)
        odd = jnp.where(low, tr, 0.0) if g == 0 else jnp.where(low, 0.0, t)
        dqp_ref[:, (2 * j) * LANES:(2 * j + 1) * LANES] = even.astype(BF16)
        dqp_ref[:, (2 * j + 1) * LANES:(2 * j + 2) * LANES] = odd.astype(BF16)

    dkv = _dot(h, w_ref[:, P_DK:P_IQ])
    dk_ref[...] = dkv[:, :LANES]
    dv_ref[...] = dkv[:, LANES:]
    dkb_ref[...] = dkv[:, :LANES].astype(BF16)
    dvb_ref[...] = dkv[:, LANES:].astype(BF16)

    iq_ref[...] = _dot(h, w_ref[:, P_IQ:P_IK]).astype(BF16)
    ik2 = _dot(h, w_ref[:, P_IK:P_COLS])
    ik_ref[...] = ik2[:, :IDX_DIM]
    ikd_ref[...] = ik2[:, :LANES].astype(BF16)
    iw_ref[...] = ik2[:, LANES:] * (IDX_HEADS ** -0.5) * IDX_SCALE


def _proj(x, mod, norm_g, lw, cos_t, sin_t):
    b, t, d = x.shape
    r = mod.shape[1]
    tm = min(TOKEN_TILE, t)

    def mod_spec(k):
        if r == 1:
            return pl.BlockSpec((None, 1, d), lambda bi, i: (bi, 0, k))
        return pl.BlockSpec((None, tm, d), lambda bi, i: (bi, i, k))

    tok = lambda w: pl.BlockSpec((None, tm, w), lambda bi, i: (bi, i, 0))
    tab = pl.BlockSpec((tm, LANES), lambda bi, i: (i, 0))
    in_specs = [tok(d), mod_spec(3), mod_spec(4), _resident((1, d)), _resident(lw["w_in"].shape),
                _resident((1, Q_LORA)), _resident(lw["w_ab"].shape), _resident((1, KV_LORA)),
                _resident(lw["w_ukp"].shape), _resident(lw["w_uv"].shape), tab, tab]
    nq = MLA_HEADS * HEAD_TILE
    outs = [("q", nq, BF16), ("k", nq, BF16), ("v", MLA_HEADS * MLA_V, BF16), ("ckv", KV_LORA, F32),
            ("kr", MLA_ROPE, F32), ("dqp", DSA_HEADS * LANES, BF16), ("dk", LANES, F32), ("dv", LANES, F32),
            ("dkb", LANES, BF16), ("dvb", LANES, BF16), ("iq", IDX_HEADS * IDX_DIM, BF16),
            ("ik", IDX_DIM, F32), ("ikd", LANES, BF16), ("iw", LANES, F32)]
    res = pl.pallas_call(
        _proj_body,
        grid=(b, t // tm),
        in_specs=in_specs,
        out_specs=[tok(w) for _, w, _ in outs],
        out_shape=[jax.ShapeDtypeStruct((b, t, w), dt) for _, w, dt in outs],
        compiler_params=_params(("arbitrary", "arbitrary")),
        name="proj",
    )(x, mod, mod, norm_g.reshape(1, d), lw["w_in"], lw["q_norm"], lw["w_ab"], lw["kv_norm"],
      lw["w_ukp"], lw["w_uv"], cos_t, sin_t)
    return {name: val for (name, _, _), val in zip(outs, res)}


def _bucket_table(max_d):
    n = np.arange(max_d + 1, dtype=np.int32)
    max_exact = N_BUCKETS // 2
    nf = np.maximum(n, 1).astype(np.float32)
    large = max_exact + (np.log(nf / np.float32(max_exact)) / np.float32(math.log(MAX_DISTANCE / max_exact))
                         * np.float32(N_BUCKETS - max_exact)).astype(np.int32)
    large = np.minimum(large, N_BUCKETS - 1)
    return np.where(n < max_exact, n, large).astype(np.int32)


def _bias_body(rb_ref, bkt_ref, bkts_ref, bp_ref, bs_ref):
    h = pl.program_id(0)
    bkt = bkt_ref[...]
    bkts = bkts_ref[...]
    tile = jnp.zeros(bkt.shape, F32)
    row = jnp.zeros(bkts.shape, F32)
    for bi in range(N_BUCKETS):
        val = rb_ref[bi, h] * LOG2E
        tile = jnp.where(bkt == bi, val, tile)
        row = jnp.where(bkts == bi, val, row)
    bp_ref[...] = tile
    bs_ref[...] = row


def _bias_tables(rel_bias, tq, tk):
    table = _bucket_table(tq + tk)
    r = np.arange(tq)[:, None]
    c = np.arange(tk)[None, :]
    bkt_p = np.stack([table[np.clip(r - c, 0, None)], table[tq + r - c]]).astype(np.int32)
    d_last = LANES - np.arange(LANES)
    bkt_s = np.stack([table[np.minimum(d_last, tq + tk)], np.full(LANES, N_BUCKETS - 1),
                      np.zeros(LANES, np.int64)]).astype(np.int32)
    return pl.pallas_call(
        _bias_body,
        grid=(DSA_HEADS,),
        in_specs=[pl.BlockSpec(memory_space=pltpu.SMEM),
                  pl.BlockSpec(bkt_p.shape, lambda h: (0, 0, 0)),
                  pl.BlockSpec(bkt_s.shape, lambda h: (0, 0))],
        out_specs=[pl.BlockSpec((None,) + bkt_p.shape, lambda h: (h, 0, 0, 0)),
                   pl.BlockSpec((None,) + bkt_s.shape, lambda h: (h, 0, 0))],
        out_shape=[jax.ShapeDtypeStruct((DSA_HEADS,) + bkt_p.shape, F32),
                   jax.ShapeDtypeStruct((DSA_HEADS,) + bkt_s.shape, F32)],
        compiler_params=_params(("arbitrary",)),
        name="bias_tables",
    )(rel_bias, jnp.asarray(bkt_p), jnp.asarray(bkt_s))


def _lane_tiles(x):
    return [x[:, j * LANES:(j + 1) * LANES] for j in range(x.shape[1] // LANES)]


def _head_step(t, vt, m_ref, l_ref, head, shift=None):
    tiles = _lane_tiles(t)
    fold = tiles[0]
    for x in tiles[1:]:
        fold = jnp.maximum(fold, x)
    rmax = jnp.max(fold, axis=-1, keepdims=True)
    m_old = m_ref[head]
    if shift is None:
        m_new = jnp.maximum(m_old, rmax)
        ref = m_new
    else:
        m_new = jnp.maximum(m_old, rmax + shift)
        ref = m_new - shift
    alpha = jnp.exp2(m_old - m_new)
    ps = [jnp.exp2(x - ref) for x in tiles]
    psum = ps[0]
    for x in ps[1:]:
        psum = psum + x
    l_ref[head] = alpha * l_ref[head] + psum
    m_ref[head] = m_new
    p = jnp.concatenate([x.astype(BF16) for x in ps], axis=1)
    return alpha, _dot(p, vt)


def _pair_update(acc_ref, pair, res0, res1, low):
    acc_ref[pair] = (acc_ref[pair] * jnp.where(low, res0[0], res1[0]) + jnp.where(low, res0[1], res1[1]))


def _softmax_init(m_ref, l_ref, acc_ref):
    m_ref[...] = jnp.full(m_ref.shape, -jnp.inf, F32)
    l_ref[...] = jnp.zeros(l_ref.shape, F32)
    acc_ref[...] = jnp.zeros(acc_ref.shape, F32)


def _pair_out(l_ref, acc_ref, pair, h0, h1, low):
    inv0 = 1.0 / jnp.sum(l_ref[h0], axis=-1, keepdims=True)
    inv1 = 1.0 / jnp.sum(l_ref[h1], axis=-1, keepdims=True)
    return (acc_ref[pair] * jnp.where(low, inv0, inv1)).astype(BF16)


def _mla_prompt_body(q_ref, k_ref, v_ref, o_ref, m_ref, l_ref, acc_ref, *, tile):
    i = pl.program_id(1)
    low = lax.broadcasted_iota(jnp.int32, (tile, LANES), 1) < MLA_V
    row = lax.broadcasted_iota(jnp.int32, (tile, tile), 0)
    col = lax.broadcasted_iota(jnp.int32, (tile, tile), 1)
    scale = MLA_SCALE * LOG2E
    _softmax_init(m_ref, l_ref, acc_ref)

    def step(start, width, causal):
        ks = pl.ds(pl.multiple_of(start, tile), width)
        for pair in range(MLA_HEADS // 2):
            res = []
            for head in (2 * pair, 2 * pair + 1):
                lo = head * HEAD_TILE
                t = _dot_nt(q_ref[:, lo:lo + HEAD_TILE], k_ref[ks, lo:lo + HEAD_TILE]) * scale
                if causal:
                    t = jnp.where(col <= row, t, -jnp.inf)
                res.append(_head_step(t, v_ref[ks, pair * LANES:(pair + 1) * LANES], m_ref, l_ref, head))
            _pair_update(acc_ref, pair, res[0], res[1], low)

    def wide(j, carry):
        step(j * (2 * tile), 2 * tile, False)
        return carry

    lax.fori_loop(0, i // 2, wide, 0)

    @pl.when(i % 2 == 1)
    def _():
        step((i - 1) * tile, tile, False)

    step(i * tile, tile, True)
    for pair in range(MLA_HEADS // 2):
        o_ref[:, pair * LANES:(pair + 1) * LANES] = _pair_out(l_ref, acc_ref, pair, 2 * pair, 2 * pair + 1, low)


def _mla_prompt(q, k, v):
    b, t, nq = q.shape
    tile = min(ATT_TILE, t)
    nv = v.shape[-1]
    return pl.pallas_call(
        functools.partial(_mla_prompt_body, tile=tile),
        grid=(b, t // tile),
        in_specs=[pl.BlockSpec((None, tile, nq), lambda bi, i: (bi, i, 0)),
                  pl.BlockSpec((None, t, nq), lambda bi, i: (bi, 0, 0)),
                  pl.BlockSpec((None, t, nv), lambda bi, i: (bi, 0, 0))],
        out_specs=pl.BlockSpec((None, tile, nv), lambda bi, i: (bi, i, 0)),
        out_shape=jax.ShapeDtypeStruct((b, t, nv), BF16),
        scratch_shapes=[pltpu.VMEM((MLA_HEADS, tile, LANES), F32), pltpu.VMEM((MLA_HEADS, tile, LANES), F32),
                        pltpu.VMEM((MLA_HEADS // 2, tile, LANES), F32)],
        compiler_params=_params(("arbitrary", "arbitrary")),
        name="mla_prompt",
    )(q, k, v)


INT_MIN = -2 ** 31


def _key_to_f32(key):
    bits = jnp.where(key < 0, key ^ jnp.int32(0x7FFFFFFF), key)
    return lax.bitcast_convert_type(bits, F32)


def _kth_largest(count_ge, k, rows, settled):
    cnt0 = count_ge(jnp.zeros((rows, 1), F32))
    pos = cnt0 >= k
    cur0 = jnp.where(pos, jnp.int32(0), jnp.int32(INT_MIN))
    cnt0 = jnp.where(pos, cnt0, jnp.int32(2 ** 30))

    def cond(state):
        it, _, cnt = state
        open_rows = jnp.where(settled | (cnt == k), 0.0, 1.0)
        return (it < 31) & (jnp.max(open_rows) > 0.5)

    def body(state):
        it, cur, cnt = state
        cand = cur + lax.shift_left(jnp.int32(1), jnp.int32(30) - it)
        c = count_ge(_key_to_f32(cand))
        ok = c >= k
        return it + 1, jnp.where(ok, cand, cur), jnp.where(ok, c, cnt)

    _, cur, cnt = lax.while_loop(cond, body, (jnp.int32(0), cur0, cnt0))
    return _key_to_f32(cur), cnt


def _tie_cut(count_eq_below, need, n_cols, rows):
    bits = max(1, int(math.ceil(math.log2(n_cols))))

    def body(it, cur):
        cand = cur + lax.shift_left(jnp.int32(1), jnp.int32(bits - 1) - it)
        return jnp.where(count_eq_below(cand) < need, cand, cur)

    return lax.fori_loop(0, bits, body, jnp.zeros((rows, 1), jnp.int32))


def _dsa_prompt_body(iq_ref, iw_ref, dqp_ref, ikd_ref, dkb_ref, dvb_ref, bias_ref, far_ref, o_ref,
                     s_ref, iqp_ref, wb_ref, cut_ref, cand_ref, pint_ref, m_ref, l_ref, acc_ref, *, tile, k_sel):
    i = pl.program_id(1)
    lane = lax.broadcasted_iota(jnp.int32, (tile, LANES), 1)
    low = lane < DSA_HEAD_DIM
    row = lax.broadcasted_iota(jnp.int32, (tile, tile), 0)
    col = lax.broadcasted_iota(jnp.int32, (tile, tile), 1)
    reps = tile // LANES

    for h in range(IDX_HEADS):
        t = iq_ref[:, (h // 2) * LANES:(h // 2 + 1) * LANES]
        keep = low if h % 2 == 0 else jnp.logical_not(low)
        iqp_ref[h] = jnp.where(keep, t, jnp.zeros_like(t))
        wb_ref[h] = jnp.broadcast_to(iw_ref[:, h:h + 1], (tile, LANES))

    def score_block(kb, causal):
        ks = pl.ds(pl.multiple_of(kb * tile, tile), tile)
        kt = ikd_ref[ks, :]
        sc = jnp.zeros((tile, tile), F32)
        for h in range(IDX_HEADS):
            w = jnp.concatenate([wb_ref[h]] * reps, axis=1)
            sc = sc + jnp.maximum(_dot_nt(iqp_ref[h], kt), 0.0) * w
        if causal:
            sc = jnp.where(col <= row, sc, -jnp.inf)
        s_ref[kb] = sc

    def score_loop(kb, carry):
        score_block(kb, False)
        return carry

    lax.fori_loop(0, i, score_loop, 0)
    score_block(i, True)
    s_ref[i + 1] = jnp.full((tile, tile), -jnp.inf, F32)

    def lane_fold(x):
        out = x[:, :LANES]
        for r in range(1, reps):
            out = out + x[:, r * LANES:(r + 1) * LANES]
        return out

    n_pairs = i // 2 + 1

    half = tile // 2
    col_h = lax.broadcasted_iota(jnp.int32, (half, tile), 1)

    def masked_count(pred):
        accs = []
        for r0 in range(0, tile, half):
            rows = slice(r0, r0 + half)

            def body(j, acc, rows=rows):
                for kb in (2 * j, 2 * j + 1):
                    acc = acc + lane_fold(jnp.where(pred(s_ref[kb, rows, :], kb, rows), 1.0, 0.0))
                return acc

            accs.append(lax.fori_loop(0, n_pairs, body, jnp.zeros((half, LANES), F32)))
        return jnp.sum(jnp.concatenate(accs, axis=0), axis=-1, keepdims=True).astype(jnp.int32)

    rowval = lambda ref, rows: jnp.concatenate([ref[rows, :]] * reps, axis=1)

    def count_ge(cand):
        cand_ref[...] = jnp.broadcast_to(cand, (tile, LANES))
        return masked_count(lambda s, kb, rows: s >= rowval(cand_ref, rows))

    def count_tied_below(p):
        pint_ref[...] = jnp.broadcast_to(p, (tile, LANES))
        return masked_count(lambda s, kb, rows: (s == rowval(cand_ref, rows))
                            & (kb * tile + col_h < rowval(pint_ref, rows)))

    n_adm = i * tile + lax.broadcasted_iota(jnp.int32, (tile, 1), 0) + 1
    keep_all = n_adm <= k_sel
    thr, n_ge = _kth_largest(count_ge, k_sel, tile, keep_all)
    thr = jnp.where(keep_all, F32_LOWEST, thr)
    tied = jnp.logical_not(keep_all) & (n_ge > k_sel)
    cut_ref[...] = jnp.full(cut_ref.shape, 2 ** 30, jnp.int32)

    @pl.when(jnp.max(jnp.where(tied, 1.0, 0.0)) > 0.5)
    def _():
        cand_ref[...] = jnp.broadcast_to(thr, (tile, LANES))
        need = k_sel - masked_count(lambda s, kb, rows: s > rowval(cand_ref, rows))
        cut_ref[...] = _tie_cut(count_tied_below, need, (s_ref.shape[0] - 1) * tile, tile)

    cut = cut_ref[...]

    def mask_loop(j, carry):
        for kb in (2 * j, 2 * j + 1):
            s = s_ref[kb]
            sel = (s > thr) | ((s == thr) & (kb * tile + col <= cut))
            s_ref[kb] = jnp.where(sel, 0.0, NEG)
        return carry

    lax.fori_loop(0, n_pairs, mask_loop, 0)

    _softmax_init(m_ref, l_ref, acc_ref)

    def step(kb, nblk, kind):
        ks = pl.ds(pl.multiple_of(kb * tile, tile), nblk * tile)
        kt = dkb_ref[ks, :]
        vt = dvb_ref[ks, :]
        mb = jnp.concatenate([s_ref[kb + n] for n in range(nblk)], axis=1)
        for pair in range(DSA_GROUP):
            res = []
            for head in (pair, pair + DSA_GROUP):
                s = _dot_nt(dqp_ref[:, head * LANES:(head + 1) * LANES], kt) * LOG2E + mb
                if kind == "far":
                    res.append(_head_step(s, vt, m_ref, l_ref, head, shift=far_ref[head:head + 1, :]))
                else:
                    res.append(_head_step(s + bias_ref[head, 0 if kind == "diag" else 1], vt, m_ref, l_ref, head))
            _pair_update(acc_ref, pair, res[0], res[1], low)

    n_far = jnp.maximum(i - 1, 0)

    def far_wide(j, carry):
        step(2 * j, 2, "far")
        return carry

    lax.fori_loop(0, n_far // 2, far_wide, 0)

    @pl.when(n_far % 2 == 1)
    def _():
        step(n_far - 1, 1, "far")

    @pl.when(i >= 1)
    def _():
        step(i - 1, 1, "prev")

    step(i, 1, "diag")
    for pair in range(DSA_GROUP):
        o_ref[:, pair * LANES:(pair + 1) * LANES] = _pair_out(l_ref, acc_ref, pair, pair, pair + DSA_GROUP, low)


def _dsa_prompt(pr, bias_p, bias_far, k_sel):
    iq, iw, dqp, ikd, dkb, dvb = pr["iq"], pr["iw"], pr["dqp"], pr["ikd"], pr["dkb"], pr["dvb"]
    b, t, _ = iq.shape
    tile = min(ATT_TILE, t)
    assert bias_p.shape[-1] == tile and tile >= MAX_DISTANCE
    blk = lambda w: pl.BlockSpec((None, tile, w), lambda bi, i: (bi, i, 0))
    seq = lambda w: pl.BlockSpec((None, t, w), lambda bi, i: (bi, 0, 0))
    return pl.pallas_call(
        functools.partial(_dsa_prompt_body, tile=tile, k_sel=k_sel),
        grid=(b, t // tile),
        in_specs=[blk(iq.shape[-1]), blk(LANES), blk(dqp.shape[-1]), seq(LANES), seq(LANES), seq(LANES),
                  _resident(bias_p.shape), _resident(bias_far.shape)],
        out_specs=blk(DSA_HEADS * DSA_HEAD_DIM),
        out_shape=jax.ShapeDtypeStruct((b, t, DSA_HEADS * DSA_HEAD_DIM), BF16),
        scratch_shapes=[pltpu.VMEM((t // tile + 1, tile, tile), F32),
                        pltpu.VMEM((IDX_HEADS, tile, LANES), BF16),
                        pltpu.VMEM((IDX_HEADS, tile, LANES), F32),
                        pltpu.VMEM((tile, 1), jnp.int32),
                        pltpu.VMEM((tile, LANES), F32),
                        pltpu.VMEM((tile, LANES), jnp.int32),
                        pltpu.VMEM((DSA_HEADS, tile, LANES), F32), pltpu.VMEM((DSA_HEADS, tile, LANES), F32),
                        pltpu.VMEM((DSA_GROUP, tile, LANES), F32)],
        compiler_params=_params(("arbitrary", "arbitrary")),
        name="dsa_prompt",
    )(iq, iw, dqp, ikd, dkb, dvb, bias_p, bias_far)


def _sample_q_body(q_ref, wukt_ref, ql_ref, qr_ref):
    for h in range(MLA_HEADS):
        qt = q_ref[:, h * HEAD_TILE:(h + 1) * HEAD_TILE]
        ql_ref[:, h * KV_LORA:(h + 1) * KV_LORA] = _dot(qt, wukt_ref[h]).astype(BF16)
        qr = pltpu.roll(qt.astype(F32), LANES - ROPE_LANE0, axis=1)[:, :MLA_ROPE]
        qr_ref[:, h * MLA_ROPE:(h + 1) * MLA_ROPE] = qr.astype(BF16)


def _sample_q(q, w_ukt):
    n = q.shape[0]
    return pl.pallas_call(
        _sample_q_body,
        out_shape=[jax.ShapeDtypeStruct((n, MLA_HEADS * KV_LORA), BF16),
                   jax.ShapeDtypeStruct((n, MLA_HEADS * MLA_ROPE), BF16)],
        compiler_params=pltpu.CompilerParams(vmem_limit_bytes=VMEM_LIMIT_BYTES),
        name="sample_q",
    )(q, w_ukt)


def _page_pipeline(pt_ref, caches, bufs, sems, layer, g_pages):
    b, c = pl.program_id(0), pl.program_id(1)
    nb, nc = pl.num_programs(0), pl.num_programs(1)
    step = b * nc + c
    slot = step % 2
    last = step == nb * nc - 1

    def page_copy(k, page, sl, g):
        return pltpu.make_async_copy(caches[k].at[layer, page], bufs[k].at[sl, g], sems.at[sl, k])

    def start_page(bb, cc, sl, g):
        page = pt_ref[bb, cc * g_pages + g]
        for k in range(len(caches)):
            page_copy(k, page, sl, g).start()

    def wait_all(sl):
        for g in range(g_pages):
            for k in range(len(caches)):
                page_copy(k, 0, sl, g).wait()

    @pl.when(step == 0)
    def _():
        for g in range(g_pages):
            start_page(0, 0, 0, g)

    wait_all(slot)
    wrap = c + 1 == nc
    nxt_b = jnp.where(last, 0, jnp.where(wrap, b + 1, b))
    nxt_c = jnp.where(wrap, 0, c + 1)

    def prefetch(g):
        start_page(nxt_b, nxt_c, 1 - slot, g)

    def drain():
        @pl.when(last)
        def _():
            wait_all(1 - slot)

    return slot, prefetch, drain


def _decode1_body(pt_ref, ckv_hbm, kr_hbm, ik_hbm, ql_ref, qr_ref, cn_ref, kn_ref, iq_ref, iw_ref,
                  o_ref, sc_ref, ckv_buf, kr_buf, ik_buf, sems, m_ref, l_ref, acc_ref,
                  *, layer, g_pages, scale):
    slot, prefetch, drain = _page_pipeline(pt_ref, (ckv_hbm, kr_hbm, ik_hbm), (ckv_buf, kr_buf, ik_buf),
                                           sems, layer, g_pages)
    ckv = [ckv_buf.at[slot, g] for g in range(g_pages)]
    krp = [kr_buf.at[slot, g] for g in range(g_pages)]
    ikp = [ik_buf.at[slot, g] for g in range(g_pages)]
    c = pl.program_id(1)
    nc = pl.num_programs(1)

    @pl.when(c == 0)
    def _():
        m_ref[...] = jnp.full(m_ref.shape, -jnp.inf, F32)
        l_ref[...] = jnp.zeros(l_ref.shape, F32)
        acc_ref[...] = jnp.zeros(acc_ref.shape, F32)

    ql = ql_ref[...].astype(F32)
    qr = qr_ref[...].astype(F32)
    iq = iq_ref[...].astype(F32)
    iw = iw_ref[...]
    logits, scores = [], []
    for g in range(g_pages):
        prefetch(g)
        logits.append(_dot_nt(ql, ckv[g][...]) + _dot(qr, krp[g][...]))
        s = jnp.maximum(_dot(iq, ikp[g][...]), 0.0) * iw
        scores.append(jnp.sum(s, axis=0, keepdims=True))
    sc_ref[...] = jnp.concatenate(scores, axis=1)
    t = jnp.concatenate(logits, axis=1) * scale
    m_old = m_ref[...]
    m_new = jnp.maximum(m_old, jnp.max(t, axis=-1, keepdims=True))
    p = jnp.exp2(t - m_new)
    alpha = jnp.exp2(m_old - m_new)
    l_ref[...] = alpha * l_ref[...] + jnp.sum(p, axis=-1, keepdims=True)
    m_ref[...] = m_new
    pv = jnp.zeros(acc_ref.shape, F32)
    for g in range(g_pages):
        pv = pv + _dot(p[:, g * LANES:(g + 1) * LANES], ckv[g][...])
    acc_ref[...] = acc_ref[...] * alpha + pv

    @pl.when(c == nc - 1)
    def _():
        cn = cn_ref[...].astype(BF16).astype(F32)
        kn = kn_ref[...].astype(BF16).astype(F32)
        ln = (jnp.sum(ql * cn, axis=-1, keepdims=True) + jnp.sum(qr * kn, axis=-1, keepdims=True)) * scale
        m_old = m_ref[...]
        m_new = jnp.maximum(m_old, ln)
        pn = jnp.exp2(ln - m_new)
        alpha = jnp.exp2(m_old - m_new)
        l = alpha * l_ref[...] + pn
        acc = acc_ref[...] * alpha + pn.astype(BF16).astype(F32) * cn
        o_ref[...] = (acc / l).astype(BF16)

    drain()


def _decode1(layer, page_table, cache_ckv, cache_krope, cache_idx_k, ql, qr, ckv_new, kr_new, iq, iw):
    bd, n_pages = page_table.shape
    g_pages = min(PAGES_PER_STEP, n_pages)
    nc = n_pages // g_pages
    gw = g_pages * LANES

    per_seq = lambda r, w: pl.BlockSpec((None, r, w), lambda b, c, pt: (b, 0, 0))
    hbm = pl.BlockSpec(memory_space=pl.ANY)
    grid_spec = pltpu.PrefetchScalarGridSpec(
        num_scalar_prefetch=1,
        grid=(bd, nc),
        in_specs=[hbm, hbm, hbm,
                  per_seq(MLA_HEADS, KV_LORA), per_seq(MLA_HEADS, MLA_ROPE), per_seq(1, KV_LORA),
                  per_seq(1, MLA_ROPE), per_seq(IDX_HEADS, IDX_DIM), per_seq(IDX_HEADS, 1)],
        out_specs=[per_seq(MLA_HEADS, KV_LORA),
                   pl.BlockSpec((None, None, 1, gw), lambda b, c, pt: (c, b, 0, 0))],
        scratch_shapes=[pltpu.VMEM((2, g_pages, LANES, KV_LORA), F32),
                        pltpu.VMEM((2, g_pages, MLA_ROPE, LANES), F32),
                        pltpu.VMEM((2, g_pages, IDX_DIM, LANES), F32),
                        pltpu.SemaphoreType.DMA((2, 3)),
                        pltpu.VMEM((MLA_HEADS, 1), F32), pltpu.VMEM((MLA_HEADS, 1), F32),
                        pltpu.VMEM((MLA_HEADS, KV_LORA), F32)],
    )
    o_lat, scores = pl.pallas_call(
        functools.partial(_decode1_body, layer=layer, g_pages=g_pages, scale=MLA_SCALE * LOG2E),
        grid_spec=grid_spec,
        out_shape=[jax.ShapeDtypeStruct((bd, MLA_HEADS, KV_LORA), BF16),
                   jax.ShapeDtypeStruct((nc, bd, 1, gw), F32)],
        compiler_params=_params(("arbitrary", "arbitrary")),
        name="decode_mla_idx",
    )(page_table, cache_ckv, cache_krope, cache_idx_k, ql, qr, ckv_new, kr_new, iq, iw)
    return o_lat, scores.reshape(nc, bd, gw)


def _sample_topk_body(s_ref, iq_ref, ikd_ref, iw_ref, mb_ref, mbn_ref, *, k_sel):
    nc, rows, gw = s_ref.shape
    col = lax.broadcasted_iota(jnp.int32, (rows, gw), 1)
    reps = gw // LANES

    prod = iq_ref[...].astype(F32) * jnp.concatenate([ikd_ref[...].astype(F32)] * (IDX_HEADS // 2), axis=1)
    s_new = jnp.zeros((rows, 1), F32)
    for h in range(IDX_HEADS):
        dot_h = jnp.sum(prod[:, h * IDX_DIM:(h + 1) * IDX_DIM], axis=-1, keepdims=True)
        s_new = s_new + jnp.maximum(dot_h, 0.0) * iw_ref[:, h:h + 1]

    def lane_fold(x):
        out = x[:, :LANES]
        for r in range(1, reps):
            out = out + x[:, r * LANES:(r + 1) * LANES]
        return out

    def masked_count(pred, pred_new):
        acc = jnp.zeros((rows, LANES), F32)
        for c in range(nc):
            acc = acc + lane_fold(jnp.where(pred(s_ref[c], c), 1.0, 0.0))
        cnt = jnp.sum(acc, axis=-1, keepdims=True) + jnp.where(pred_new, 1.0, 0.0)
        return cnt.astype(jnp.int32)

    n_cols = nc * gw
    thr, _ = _kth_largest(lambda cand: masked_count(lambda s, c: s >= cand, s_new >= cand), k_sel, rows,
                          jnp.zeros((rows, 1), jnp.bool_))
    need = k_sel - masked_count(lambda s, c: s > thr, s_new > thr)
    cut = _tie_cut(lambda p: masked_count(lambda s, c: (s == thr) & (c * gw + col < p),
                                          (s_new == thr) & (n_cols < p)),
                   need, n_cols + 1, rows)
    for c in range(nc):
        s = s_ref[c]
        sel = (s > thr) | ((s == thr) & (c * gw + col <= cut))
        mb_ref[c] = jnp.where(sel, 0.0, NEG)
    sel_new = (s_new > thr) | ((s_new == thr) & (n_cols <= cut))
    mbn_ref[...] = jnp.broadcast_to(jnp.where(sel_new, 0.0, NEG), mbn_ref.shape)


def _sample_topk(scores, iq, ikd, iw, k_sel):
    nc, bd, gw = scores.shape
    rows = min(32, bd)
    return pl.pallas_call(
        functools.partial(_sample_topk_body, k_sel=k_sel),
        grid=(bd // rows,),
        in_specs=[pl.BlockSpec((nc, rows, gw), lambda r: (0, r, 0)),
                  pl.BlockSpec((rows, iq.shape[-1]), lambda r: (r, 0)),
                  pl.BlockSpec((rows, LANES), lambda r: (r, 0)),
                  pl.BlockSpec((rows, LANES), lambda r: (r, 0))],
        out_specs=[pl.BlockSpec((nc, rows, gw), lambda r: (0, r, 0)),
                   pl.BlockSpec((rows, LANES), lambda r: (r, 0))],
        out_shape=[jax.ShapeDtypeStruct((nc, bd, gw), F32), jax.ShapeDtypeStruct((bd, LANES), F32)],
        compiler_params=_params(("arbitrary",)),
        name="sample_topk",
    )(scores, iq, ikd, iw)


def _decode2_body(pt_ref, k_hbm, v_hbm, q_ref, mb_ref, mbn_ref, bias_ref, dkn_ref, dvn_ref,
                  o_ref, k_buf, v_buf, sems, m_ref, l_ref, acc_ref, *, layer, g_pages):
    slot, prefetch, drain = _page_pipeline(pt_ref, (k_hbm, v_hbm), (k_buf, v_buf), sems, layer, g_pages)
    kp = [k_buf.at[slot, g] for g in range(g_pages)]
    vp = [v_buf.at[slot, g] for g in range(g_pages)]
    c = pl.program_id(1)
    nc = pl.num_programs(1)

    @pl.when(c == 0)
    def _():
        m_ref[...] = jnp.full(m_ref.shape, -jnp.inf, F32)
        l_ref[...] = jnp.zeros(l_ref.shape, F32)
        acc_ref[...] = jnp.zeros(acc_ref.shape, F32)

    q = q_ref[...].astype(F32)
    mb = mb_ref[...]
    far = bias_ref[1]
    last = jnp.where(c == nc - 1, bias_ref[0], far)
    logits = []
    for g in range(g_pages):
        prefetch(g)
        bias = last if g == g_pages - 1 else far
        logits.append(_dot(q, kp[g][...]) * LOG2E + (bias + mb[:, g * LANES:(g + 1) * LANES]))
    t = jnp.concatenate(logits, axis=1)
    m_old = m_ref[...]
    m_new = jnp.maximum(m_old, jnp.max(t, axis=-1, keepdims=True))
    p = jnp.exp2(t - m_new)
    alpha = jnp.exp2(m_old - m_new)
    l_ref[...] = alpha * l_ref[...] + jnp.sum(p, axis=-1, keepdims=True)
    m_ref[...] = m_new
    pv = jnp.zeros(acc_ref.shape, F32)
    for g in range(g_pages):
        pv = pv + _dot_nt(p[:, g * LANES:(g + 1) * LANES], vp[g][...])
    acc_ref[...] = acc_ref[...] * alpha + pv

    @pl.when(c == nc - 1)
    def _():
        dkn = dkn_ref[...].astype(BF16).astype(F32)
        dvn = dvn_ref[...].astype(BF16).astype(F32)
        ln = (jnp.sum(q * dkn, axis=-1, keepdims=True) * LOG2E
              + (bias_ref[2][:, 0:1] + mbn_ref[:, 0:1]))
        m_old = m_ref[...]
        m_new = jnp.maximum(m_old, ln)
        pn = jnp.exp2(ln - m_new)
        alpha = jnp.exp2(m_old - m_new)
        l = alpha * l_ref[...] + pn
        out = (acc_ref[...] * alpha + pn.astype(BF16).astype(F32) * dvn) / l
        rowi = lax.broadcasted_iota(jnp.int32, out.shape, 0)
        out = jnp.where(rowi < DSA_GROUP, out, pltpu.roll(out, DSA_HEAD_DIM, axis=1))
        o_ref[...] = out[:, :DSA_HEAD_DIM].astype(BF16)

    drain()


def _decode2(layer, page_table, cache_k, cache_v, dqp, mb, mbn, bias_s, dk_new, dv_new):
    bd, n_pages = page_table.shape
    g_pages = min(PAGES_PER_STEP, n_pages)
    nc = n_pages // g_pages
    gw = g_pages * LANES

    per_seq = lambda r, w: pl.BlockSpec((None, r, w), lambda b, c, pt: (b, 0, 0))
    hbm = pl.BlockSpec(memory_space=pl.ANY)
    grid_spec = pltpu.PrefetchScalarGridSpec(
        num_scalar_prefetch=1,
        grid=(bd, nc),
        in_specs=[hbm, hbm,
                  per_seq(DSA_HEADS, LANES),
                  pl.BlockSpec((None, None, 1, gw), lambda b, c, pt: (c, b, 0, 0)),
                  per_seq(1, LANES),
                  pl.BlockSpec(bias_s.shape, lambda b, c, pt: (0, 0, 0)),
                  per_seq(1, LANES), per_seq(1, LANES)],
        out_specs=per_seq(DSA_HEADS, DSA_HEAD_DIM),
        scratch_shapes=[pltpu.VMEM((2, g_pages, LANES, LANES), F32),
                        pltpu.VMEM((2, g_pages, LANES, LANES), F32),
                        pltpu.SemaphoreType.DMA((2, 2)),
                        pltpu.VMEM((DSA_HEADS, 1), F32), pltpu.VMEM((DSA_HEADS, 1), F32),
                        pltpu.VMEM((DSA_HEADS, LANES), F32)],
    )
    return pl.pallas_call(
        functools.partial(_decode2_body, layer=layer, g_pages=g_pages),
        grid_spec=grid_spec,
        out_shape=jax.ShapeDtypeStruct((bd, DSA_HEADS, DSA_HEAD_DIM), BF16),
        compiler_params=_params(("arbitrary", "arbitrary")),
        name="decode_dsa",
    )(page_table, cache_k, cache_v,
      dqp, mb.reshape(nc, bd, 1, gw), mbn.reshape(bd, 1, LANES), bias_s, dk_new, dv_new)


def _sample_out_body(ol_ref, w_ref, o_ref):
    o_ref[...] = _dot(ol_ref[...], w_ref[...]).astype(BF16)


def _sample_out(o_lat, w_uv_bd):
    n = o_lat.shape[0]
    return pl.pallas_call(
        _sample_out_body,
        out_shape=jax.ShapeDtypeStruct((n, w_uv_bd.shape[1]), BF16),
        compiler_params=pltpu.CompilerParams(vmem_limit_bytes=VMEM_LIMIT_BYTES),
        name="sample_out",
    )(o_lat, w_uv_bd)


DSA_HEAD_ORDER = tuple(h + g * DSA_GROUP for h in range(DSA_GROUP) for g in range(DSA_KV_HEADS))


def _layer_weights(l, ffn_w_gate, ffn_w_up, ffn_w_down, w_in, mla_q_norm, mla_w_uq, mla_kv_norm,
                   mla_w_uk, mla_w_uv, w_out):
    d = w_in.shape[1]
    z = lambda n: jnp.zeros((d, n), F32)
    cq, ckv, kr, dq, dk, dv, iq, ik, iw = jnp.split(w_in[l], np.cumsum(IN_SPLITS)[:-1], axis=1)
    half = MLA_ROPE // 2
    kr_rot = jnp.concatenate([-kr[:, half:], kr[:, :half]], axis=1)
    pad_rope = lambda a: jnp.concatenate([z(ROPE_LANE0), a, z(LANES - ROPE_LANE0 - MLA_ROPE)], axis=1)
    w_in_p = jnp.concatenate([cq, ckv, pad_rope(kr), pad_rope(kr_rot), dq, dk, dv, iq, ik, ik,
                              iw, z(LANES - IDX_HEADS)], axis=1).astype(BF16)
    assert w_in_p.shape[1] == P_COLS

    uq = mla_w_uq[l].reshape(Q_LORA, MLA_HEADS, MLA_NOPE + MLA_ROPE)
    nope, r1, r2 = uq[..., :MLA_NOPE], uq[..., MLA_NOPE:MLA_NOPE + half], uq[..., MLA_NOPE + half:]
    zq = lambda n: jnp.zeros((Q_LORA, MLA_HEADS, n), F32)
    tail = HEAD_TILE - MLA_NOPE - MLA_ROPE
    w_a = jnp.concatenate([nope, r1, r2, zq(tail)], axis=-1).reshape(Q_LORA, -1)
    w_b = jnp.concatenate([zq(MLA_NOPE), -r2, r1, zq(tail)], axis=-1).reshape(Q_LORA, -1)
    w_ab = jnp.concatenate([w_a, w_b], axis=1).astype(BF16)

    uk = mla_w_uk[l]
    w_ukp = jnp.concatenate([uk, jnp.zeros((KV_LORA, MLA_HEADS, HEAD_TILE - MLA_NOPE), F32)],
                            axis=-1).reshape(KV_LORA, -1).astype(BF16)
    w_ukt = jnp.concatenate([jnp.transpose(uk, (1, 2, 0)),
                             jnp.zeros((MLA_HEADS, HEAD_TILE - MLA_NOPE, KV_LORA), F32)], axis=1).astype(BF16)
    uv = mla_w_uv[l]
    w_uv = uv.reshape(KV_LORA, -1).astype(BF16)
    eye = jnp.eye(MLA_HEADS, dtype=F32)
    w_uv_bd = (jnp.transpose(uv, (1, 0, 2))[:, :, None, :] * eye[:, None, :, None]).reshape(
        MLA_HEADS * KV_LORA, MLA_HEADS * MLA_V).astype(BF16)

    mla_w = MLA_HEADS * MLA_V
    wo = w_out[l]
    wo_dsa = wo[mla_w:].reshape(DSA_HEADS, DSA_HEAD_DIM, -1)[jnp.asarray(DSA_HEAD_ORDER)]
    wo_p = jnp.concatenate([wo[:mla_w], wo_dsa.reshape(DSA_HEADS * DSA_HEAD_DIM, -1)], axis=0).astype(BF16)

    nch = D_FF // FF_CHUNK
    chunk_cols = lambda w: jnp.transpose(w.reshape(d, nch, FF_CHUNK), (1, 0, 2)).astype(BF16)
    ffn = [(chunk_cols(ffn_w_gate[l, j]), chunk_cols(ffn_w_up[l, j]),
            ffn_w_down[l, j].reshape(nch, FF_CHUNK, d).astype(BF16)) for j in range(2)]
    return dict(w_in=w_in_p, q_norm=mla_q_norm[l].reshape(1, -1), w_ab=w_ab,
                kv_norm=mla_kv_norm[l].reshape(1, -1), w_ukp=w_ukp, w_ukt=w_ukt, w_uv=w_uv,
                w_uv_bd=w_uv_bd, w_out=wo_p, ffn=ffn)


def _rope_tables(pos):
    half = MLA_ROPE // 2
    inv = ROPE_THETA ** (-jnp.arange(half, dtype=F32) / half)
    ang = pos.astype(F32)[:, None] * inv[None, :]
    n = pos.shape[0]
    tail = jnp.zeros((n, LANES - ROPE_LANE0 - MLA_ROPE), F32)
    cos_t = jnp.concatenate([jnp.ones((n, ROPE_LANE0), F32), jnp.cos(ang), jnp.cos(ang), tail], axis=1)
    sin_t = jnp.concatenate([jnp.zeros((n, ROPE_LANE0), F32), jnp.sin(ang), jnp.sin(ang), tail], axis=1)
    return cos_t, sin_t


def kernel(x_prompt, x_sample, cache_ckv, cache_krope, cache_k, cache_v, cache_idx_k, page_table,
           c_prompt, c_sample, norm_g, w_ada, b_ada, ffn_w_gate, ffn_w_up, ffn_w_down, w_in,
           mla_q_norm, mla_w_uq, mla_kv_norm, mla_w_uk, mla_w_uv, w_out, rel_bias, final_norm):
    b, t, d = x_prompt.shape
    bd, dec_seq, _ = x_sample.shape
    depth = w_in.shape[0]
    n_pages = page_table.shape[1]
    page = cache_ckv.shape[2]
    past = n_pages * page
    assert dec_seq == 1 and page == LANES and d == D_MODEL
    assert t % min(ATT_TILE, t) == 0 and n_pages % min(PAGES_PER_STEP, n_pages) == 0
    k_prompt = min(TOPK_MAX, t // 4)
    k_sample = min(TOPK_MAX, (past + dec_seq) // 4)

    mod = _adaln(jnp.concatenate([c_prompt, c_sample], axis=0), w_ada, b_ada)
    cos_p, sin_p = _rope_tables(jnp.arange(t, dtype=jnp.int32))
    cos_s, sin_s = _rope_tables(jnp.full((bd,), past, dtype=jnp.int32))
    tile = min(ATT_TILE, t)
    bias_p, bias_s = _bias_tables(rel_bias, tile, tile)
    bias_s = jnp.transpose(bias_s, (1, 0, 2))
    slot_minor = lambda a: jnp.swapaxes(a, 2, 3)
    kv_slot_minor = lambda a: jnp.transpose(a, (0, 1, 3, 4, 2)).reshape(a.shape[:2] + (LANES, page))
    krope_t, idx_k_t = slot_minor(cache_krope), slot_minor(cache_idx_k)
    ck, cv = kv_slot_minor(cache_k), kv_slot_minor(cache_v)
    head_order = jnp.asarray(DSA_HEAD_ORDER)

    hp = x_prompt
    hs = x_sample.reshape(1, bd, d)
    rows_p, rows_s = [], []
    for l in range(depth):
        lw = _layer_weights(l, ffn_w_gate, ffn_w_up, ffn_w_down, w_in, mla_q_norm, mla_w_uq,
                            mla_kv_norm, mla_w_uk, mla_w_uv, w_out)
        mod_p = mod[l, :b].reshape(b, 1, -1)
        mod_s = mod[l, b:].reshape(1, bd, -1)
        last = l == depth - 1

        hp = _ffn(hp, mod_p, 0, norm_g[l, 0], *lw["ffn"][0])
        pr = _proj(hp, mod_p, norm_g[l, 1], lw, cos_p, sin_p)
        a_p = _mla_prompt(pr["q"], pr["k"], pr["v"])
        b_p = _dsa_prompt(pr, bias_p, bias_s[1], k_prompt)
        hp = _ffn(hp, mod_p, 6, norm_g[l, 2], *lw["ffn"][1], mix=(a_p, b_p, lw["w_out"], 5),
                  final_g=final_norm if last else None)
        rows_p.append((pr["ckv"], pr["kr"], pr["dk"], pr["dv"], pr["ik"]))

        hs = _ffn(hs, mod_s, 0, norm_g[l, 0], *lw["ffn"][0])
        sr = {name: val[0] for name, val in _proj(hs, mod_s, norm_g[l, 1], lw, cos_s, sin_s).items()}
        ql, qr = _sample_q(sr["q"], lw["w_ukt"])
        o_lat, scores = _decode1(
            l, page_table, cache_ckv, krope_t, idx_k_t,
            ql.reshape(bd, MLA_HEADS, KV_LORA), qr.reshape(bd, MLA_HEADS, MLA_ROPE),
            sr["ckv"].reshape(bd, 1, KV_LORA), sr["kr"].reshape(bd, 1, MLA_ROPE),
            sr["iq"].reshape(bd, IDX_HEADS, IDX_DIM), sr["iw"][:, :IDX_HEADS].reshape(bd, IDX_HEADS, 1))
        mb, mbn = _sample_topk(scores, sr["iq"], sr["ikd"], sr["iw"], k_sample)
        o_dsa = _decode2(l, page_table, ck, cv, sr["dqp"].reshape(bd, DSA_HEADS, LANES), mb, mbn, bias_s,
                         sr["dk"].reshape(bd, 1, LANES), sr["dv"].reshape(bd, 1, LANES))
        a_s = _sample_out(o_lat.reshape(bd, MLA_HEADS * KV_LORA), lw["w_uv_bd"])
        b_s = o_dsa[:, head_order].reshape(bd, DSA_HEADS * DSA_HEAD_DIM)
        hs = _ffn(hs, mod_s, 6, norm_g[l, 2], *lw["ffn"][1],
                  mix=(a_s.reshape(1, bd, -1), b_s.reshape(1, bd, -1), lw["w_out"], 5),
                  final_g=final_norm if last else None)
        rows_s.append((sr["ckv"], sr["kr"], sr["dk"], sr["dv"], sr["ik"]))

    p_ckv, p_kr, p_k, p_v, p_ik = [jnp.stack(x, axis=0) for x in zip(*rows_p)]
    s_ckv, s_kr, s_k, s_v, s_ik = [jnp.stack(x, axis=0) for x in zip(*rows_s)]
    kv5 = lambda a, n: a.reshape(depth, n[0], n[1], DSA_KV_HEADS, DSA_HEAD_DIM)
    seq4 = lambda a: a.reshape(depth, bd, dec_seq, a.shape[-1])
    return (hp, hs.reshape(bd, dec_seq, d), p_ckv, p_kr, kv5(p_k, (b, t)), kv5(p_v, (b, t)), p_ik,
            seq4(s_ckv), seq4(s_kr), kv5(s_k, (bd, dec_seq)), kv5(s_v, (bd, dec_seq)), seq4(s_ik))
```

```python
import functools
import math

import numpy as np
import jax
import jax.numpy as jnp
from jax import lax
from jax.experimental import pallas as pl
from jax.experimental.pallas import tpu as pltpu

D_MODEL = 1024
MLA_HEADS = 8
MLA_NOPE = 64
MLA_ROPE = 32
MLA_V = 64
Q_LORA = 384
KV_LORA = 256
DSA_HEADS = 8
DSA_KV_HEADS = 2
DSA_HEAD_DIM = 64
DSA_GROUP = DSA_HEADS // DSA_KV_HEADS
IDX_HEADS = 8
IDX_DIM = 64
TOPK_MAX = 256
N_BUCKETS = 32
MAX_DISTANCE = 128
D_FF = 2816
N_MOD = 9
ROPE_THETA = 10000.0
EPS = 1e-6
MLA_SCALE = (MLA_NOPE + MLA_ROPE) ** -0.5
DSA_SCALE = DSA_HEAD_DIM ** -0.5
IDX_SCALE = IDX_DIM ** -0.5
IN_SPLITS = (Q_LORA, KV_LORA, MLA_ROPE, DSA_HEADS * DSA_HEAD_DIM, DSA_KV_HEADS * DSA_HEAD_DIM,
             DSA_KV_HEADS * DSA_HEAD_DIM, IDX_HEADS * IDX_DIM, IDX_DIM, IDX_HEADS)

LANES = 128
MXU_DIM = 256
VMEM_LIMIT_BYTES = 56 * 1024 * 1024

LOG2E = math.log2(math.e)
NEG = -1e30
F32_LOWEST = float(np.finfo(np.float32).min)
BF16 = jnp.bfloat16
F32 = jnp.float32

FF_CHUNK = MXU_DIM
TOKEN_TILE = 512
ATT_TILE = 256
PAGES_PER_STEP = 16


def _dot(a, b):
    return jnp.dot(a, b, preferred_element_type=F32)


def _dot_nt(a, b):
    return lax.dot_general(a, b, (((1,), (1,)), ((), ())), preferred_element_type=F32)


def _rms(x, g):
    return x * lax.rsqrt(jnp.mean(x * x, axis=-1, keepdims=True) + EPS) * g


def _params(semantics):
    return pltpu.CompilerParams(dimension_semantics=semantics, vmem_limit_bytes=VMEM_LIMIT_BYTES)


def _resident(shape):
    zeros = (0,) * len(shape)
    return pl.BlockSpec(shape, lambda *_: zeros, pipeline_mode=pl.Buffered(1))


def _adaln_body(c_ref, w_ref, b_ref, o_ref):
    o_ref[...] = _dot(c_ref[...].astype(BF16), w_ref[...].astype(BF16)) + b_ref[...]


def _adaln(c_all, w_ada, b_ada):
    depth, d, n = w_ada.shape
    r = c_all.shape[0]
    tn = 1152
    return pl.pallas_call(
        _adaln_body,
        grid=(depth, n // tn),
        in_specs=[pl.BlockSpec((r, d), lambda l, j: (0, 0)),
                  pl.BlockSpec((None, d, tn), lambda l, j: (l, 0, j)),
                  pl.BlockSpec((None, 1, tn), lambda l, j: (l, 0, j))],
        out_specs=pl.BlockSpec((None, r, tn), lambda l, j: (l, 0, j)),
        out_shape=jax.ShapeDtypeStruct((depth, r, n), F32),
        compiler_params=_params(("arbitrary", "arbitrary")),
        name="adaln",
    )(c_all, w_ada, b_ada.reshape(depth, 1, n))


def _ffn_body(*refs, with_mix, with_final):
    x_ref, sh_ref, sc_ref, g_ref, ng_ref, wg_ref, wu_ref, wd_ref = refs[:8]
    rest = list(refs[8:])
    if with_mix:
        a_ref, b_ref, wo_ref, g2_ref = rest[:4]
        rest = rest[4:]
    if with_final:
        fg_ref = rest[0]
        rest = rest[1:]
    o_ref = rest[0]

    x = x_ref[...]
    if with_mix:
        half = a_ref.shape[-1]
        mix = _dot(a_ref[...], wo_ref[:half, :]) + _dot(b_ref[...], wo_ref[half:, :])
        x = x + g2_ref[...] * mix
    h = (_rms(x, ng_ref[...]) * (1.0 + sc_ref[...]) + sh_ref[...]).astype(BF16)
    acc = jnp.zeros(x.shape, F32)
    for c in range(wg_ref.shape[0]):
        gate = _dot(h, wg_ref[c])
        up = _dot(h, wu_ref[c])
        act = (gate * (1.0 / (1.0 + jnp.exp(-gate))) * up).astype(BF16)
        acc = acc + _dot(act, wd_ref[c])
    y = x + 0.5 * g_ref[...] * acc
    if with_final:
        y = _rms(y, fg_ref[...])
    o_ref[...] = y


def _ffn(x, mod, chunk0, norm_g, wg, wu, wd, mix=None, final_g=None):
    b, t, d = x.shape
    r = mod.shape[1]
    tm = min(TOKEN_TILE, t)
    rm = 1 if r == 1 else tm

    def mod_spec(k):
        if r == 1:
            return pl.BlockSpec((None, 1, d), lambda bi, i: (bi, 0, k))
        return pl.BlockSpec((None, rm, d), lambda bi, i: (bi, i, k))

    tok = lambda w: pl.BlockSpec((None, tm, w), lambda bi, i: (bi, i, 0))
    in_specs = [tok(d), mod_spec(chunk0), mod_spec(chunk0 + 1), mod_spec(chunk0 + 2),
                _resident((1, d)), _resident(wg.shape), _resident(wu.shape), _resident(wd.shape)]
    args = [x, mod, mod, mod, norm_g.reshape(1, d), wg, wu, wd]
    if mix is not None:
        a, bb, wo, g2_chunk = mix
        in_specs += [tok(a.shape[-1]), tok(bb.shape[-1]), _resident(wo.shape), mod_spec(g2_chunk)]
        args += [a, bb, wo, mod]
    if final_g is not None:
        in_specs.append(_resident((1, d)))
        args.append(final_g.reshape(1, d))
    return pl.pallas_call(
        functools.partial(_ffn_body, with_mix=mix is not None, with_final=final_g is not None),
        grid=(b, t // tm),
        in_specs=in_specs,
        out_specs=tok(d),
        out_shape=jax.ShapeDtypeStruct((b, t, d), F32),
        compiler_params=_params(("arbitrary", "arbitrary")),
        name="ffn",
    )(*args)


P_CQ = 0
P_CKV = P_CQ + Q_LORA
P_KR = P_CKV + KV_LORA
P_KRR = P_KR + LANES
P_DQ = P_KRR + LANES
P_DK = P_DQ + DSA_HEADS * DSA_HEAD_DIM
P_DV = P_DK + LANES
P_IQ = P_DV + LANES
P_IK = P_IQ + IDX_HEADS * IDX_DIM
P_IW = P_IK + LANES
P_COLS = P_IW + LANES
HEAD_TILE = LANES
ROPE_LANE0 = MLA_NOPE


def _proj_body(x_ref, sh_ref, sc_ref, ng_ref, w_ref, qn_ref, wab_ref, kvn_ref, wuk_ref, wuv_ref,
               cos_ref, sin_ref,
               q_ref, k_ref, v_ref, ckv_ref, kr_ref, dqp_ref, dk_ref, dv_ref, dkb_ref, dvb_ref,
               iq_ref, ik_ref, ikd_ref, iw_ref):
    h = (_rms(x_ref[...], ng_ref[...]) * (1.0 + sc_ref[...]) + sh_ref[...]).astype(BF16)
    cos = cos_ref[...]
    sin = sin_ref[...]
    nq = MLA_HEADS * HEAD_TILE

    cq = _rms(_dot(h, w_ref[:, P_CQ:P_CKV]), qn_ref[...]).astype(BF16)
    ab = _dot(cq, wab_ref[...])
    for hh in range(MLA_HEADS):
        lo = hh * HEAD_TILE
        q_ref[:, lo:lo + HEAD_TILE] = (ab[:, lo:lo + HEAD_TILE] * cos
                                       + ab[:, nq + lo:nq + lo + HEAD_TILE] * sin).astype(BF16)

    ckv = _rms(_dot(h, w_ref[:, P_CKV:P_KR]), kvn_ref[...])
    ckv_ref[...] = ckv
    ckv_b = ckv.astype(BF16)
    kr2 = _dot(h, w_ref[:, P_KR:P_DQ])
    krt = kr2[:, :LANES] * cos + kr2[:, LANES:] * sin
    kr_ref[...] = pltpu.roll(krt, LANES - ROPE_LANE0, axis=1)[:, :MLA_ROPE]
    kn = _dot(ckv_b, wuk_ref[...])
    for hh in range(MLA_HEADS):
        lo = hh * HEAD_TILE
        k_ref[:, lo:lo + HEAD_TILE] = (kn[:, lo:lo + HEAD_TILE] + krt).astype(BF16)
    v_ref[...] = _dot(ckv_b, wuv_ref[...]).astype(BF16)

    dq = _dot(h, w_ref[:, P_DQ:P_DK]) * DSA_SCALE
    lane = lax.broadcasted_iota(jnp.int32, (dq.shape[0], LANES), 1)
    low = lane < DSA_HEAD_DIM
    for j in range(DSA_HEADS // 2):
        t = dq[:, j * LANES:(j + 1) * LANES]
        tr = pltpu.roll(t, DSA_HEAD_DIM, axis=1)
        g = (2 * j) // DSA_GROUP
        even = jnp.where(low, t, 0.0) if g == 0 else jnp.where(low, 0.0, tr)
        odd = jnp.where(low, tr, 0.0) if g == 0 else jnp.where(low, 0.0, t)
        dqp_ref[:, (2 * j) * LANES:(2 * j + 1) * LANES] = even.astype(BF16)
        dqp_ref[:, (2 * j + 1) * LANES:(2 * j + 2) * LANES] = odd.astype(BF16)

    dkv = _dot(h, w_ref[:, P_DK:P_IQ])
    dk_ref[...] = dkv[:, :LANES]
    dv_ref[...] = dkv[:, LANES:]
    dkb_ref[...] = dkv[:, :LANES].astype(BF16)
    dvb_ref[...] = dkv[:, LANES:].astype(BF16)

    iq_ref[...] = _dot(h, w_ref[:, P_IQ:P_IK]).astype(BF16)
    ik2 = _dot(h, w_ref[:, P_IK:P_COLS])
    ik_ref[...] = ik2[:, :IDX_DIM]
    ikd_ref[...] = ik2[:, :LANES].astype(BF16)
    iw_ref[...] = ik2[:, LANES:] * (IDX_HEADS ** -0.5) * IDX_SCALE


def _proj(x, mod, norm_g, lw, cos_t, sin_t):
    b, t, d = x.shape
    r = mod.shape[1]
    tm = min(TOKEN_TILE, t)

    def mod_spec(k):
        if r == 1:
            return pl.BlockSpec((None, 1, d), lambda bi, i: (bi, 0, k))
        return pl.BlockSpec((None, tm, d), lambda bi, i: (bi, i, k))

    tok = lambda w: pl.BlockSpec((None, tm, w), lambda bi, i: (bi, i, 0))
    tab = pl.BlockSpec((tm, LANES), lambda bi, i: (i, 0))
    in_specs = [tok(d), mod_spec(3), mod_spec(4), _resident((1, d)), _resident(lw["w_in"].shape),
                _resident((1, Q_LORA)), _resident(lw["w_ab"].shape), _resident((1, KV_LORA)),
                _resident(lw["w_ukp"].shape), _resident(lw["w_uv"].shape), tab, tab]
    nq = MLA_HEADS * HEAD_TILE
    outs = [("q", nq, BF16), ("k", nq, BF16), ("v", MLA_HEADS * MLA_V, BF16), ("ckv", KV_LORA, F32),
            ("kr", MLA_ROPE, F32), ("dqp", DSA_HEADS * LANES, BF16), ("dk", LANES, F32), ("dv", LANES, F32),
            ("dkb", LANES, BF16), ("dvb", LANES, BF16), ("iq", IDX_HEADS * IDX_DIM, BF16),
            ("ik", IDX_DIM, F32), ("ikd", LANES, BF16), ("iw", LANES, F32)]
    res = pl.pallas_call(
        _proj_body,
        grid=(b, t // tm),
        in_specs=in_specs,
        out_specs=[tok(w) for _, w, _ in outs],
        out_shape=[jax.ShapeDtypeStruct((b, t, w), dt) for _, w, dt in outs],
        compiler_params=_params(("arbitrary", "arbitrary")),
        name="proj",
    )(x, mod, mod, norm_g.reshape(1, d), lw["w_in"], lw["q_norm"], lw["w_ab"], lw["kv_norm"],
      lw["w_ukp"], lw["w_uv"], cos_t, sin_t)
    return {name: val for (name, _, _), val in zip(outs, res)}


def _bucket_table(max_d):
    n = np.arange(max_d + 1, dtype=np.int32)
    max_exact = N_BUCKETS // 2
    nf = np.maximum(n, 1).astype(np.float32)
    large = max_exact + (np.log(nf / np.float32(max_exact)) / np.float32(math.log(MAX_DISTANCE / max_exact))
                         * np.float32(N_BUCKETS - max_exact)).astype(np.int32)
    large = np.minimum(large, N_BUCKETS - 1)
    return np.where(n < max_exact, n, large).astype(np.int32)


def _bias_body(rb_ref, bkt_ref, bkts_ref, bp_ref, bs_ref):
    h = pl.program_id(0)
    bkt = bkt_ref[...]
    bkts = bkts_ref[...]
    tile = jnp.zeros(bkt.shape, F32)
    row = jnp.zeros(bkts.shape, F32)
    for bi in range(N_BUCKETS):
        val = rb_ref[bi, h] * LOG2E
        tile = jnp.where(bkt == bi, val, tile)
        row = jnp.where(bkts == bi, val, row)
    bp_ref[...] = tile
    bs_ref[...] = row


def _bias_tables(rel_bias, tq, tk):
    table = _bucket_table(tq + tk)
    r = np.arange(tq)[:, None]
    c = np.arange(tk)[None, :]
    bkt_p = np.stack([table[np.clip(r - c, 0, None)], table[tq + r - c]]).astype(np.int32)
    d_last = LANES - np.arange(LANES)
    bkt_s = np.stack([table[np.minimum(d_last, tq + tk)], np.full(LANES, N_BUCKETS - 1),
                      np.zeros(LANES, np.int64)]).astype(np.int32)
    return pl.pallas_call(
        _bias_body,
        grid=(DSA_HEADS,),
        in_specs=[pl.BlockSpec(memory_space=pltpu.SMEM),
                  pl.BlockSpec(bkt_p.shape, lambda h: (0, 0, 0)),
                  pl.BlockSpec(bkt_s.shape, lambda h: (0, 0))],
        out_specs=[pl.BlockSpec((None,) + bkt_p.shape, lambda h: (h, 0, 0, 0)),
                   pl.BlockSpec((None,) + bkt_s.shape, lambda h: (h, 0, 0))],
        out_shape=[jax.ShapeDtypeStruct((DSA_HEADS,) + bkt_p.shape, F32),
                   jax.ShapeDtypeStruct((DSA_HEADS,) + bkt_s.shape, F32)],
        compiler_params=_params(("arbitrary",)),
        name="bias_tables",
    )(rel_bias, jnp.asarray(bkt_p), jnp.asarray(bkt_s))


def _lane_tiles(x):
    return [x[:, j * LANES:(j + 1) * LANES] for j in range(x.shape[1] // LANES)]


def _head_step(t, vt, m_ref, l_ref, head, shift=None):
    tiles = _lane_tiles(t)
    fold = tiles[0]
    for x in tiles[1:]:
        fold = jnp.maximum(fold, x)
    rmax = jnp.max(fold, axis=-1, keepdims=True)
    m_old = m_ref[head]
    if shift is None:
        m_new = jnp.maximum(m_old, rmax)
        ref = m_new
    else:
        m_new = jnp.maximum(m_old, rmax + shift)
        ref = m_new - shift
    alpha = jnp.exp2(m_old - m_new)
    ps = [jnp.exp2(x - ref) for x in tiles]
    psum = ps[0]
    for x in ps[1:]:
        psum = psum + x
    l_ref[head] = alpha * l_ref[head] + psum
    m_ref[head] = m_new
    p = jnp.concatenate([x.astype(BF16) for x in ps], axis=1)
    return alpha, _dot(p, vt)


def _pair_update(acc_ref, pair, res0, res1, low):
    acc_ref[pair] = (acc_ref[pair] * jnp.where(low, res0[0], res1[0]) + jnp.where(low, res0[1], res1[1]))


def _softmax_init(m_ref, l_ref, acc_ref):
    m_ref[...] = jnp.full(m_ref.shape, -jnp.inf, F32)
    l_ref[...] = jnp.zeros(l_ref.shape, F32)
    acc_ref[...] = jnp.zeros(acc_ref.shape, F32)


def _pair_out(l_ref, acc_ref, pair, h0, h1, low):
    inv0 = 1.0 / jnp.sum(l_ref[h0], axis=-1, keepdims=True)
    inv1 = 1.0 / jnp.sum(l_ref[h1], axis=-1, keepdims=True)
    return (acc_ref[pair] * jnp.where(low, inv0, inv1)).astype(BF16)


def _mla_prompt_body(q_ref, k_ref, v_ref, o_ref, m_ref, l_ref, acc_ref, *, tile):
    i = pl.program_id(1)
    low = lax.broadcasted_iota(jnp.int32, (tile, LANES), 1) < MLA_V
    row = lax.broadcasted_iota(jnp.int32, (tile, tile), 0)
    col = lax.broadcasted_iota(jnp.int32, (tile, tile), 1)
    scale = MLA_SCALE * LOG2E
    _softmax_init(m_ref, l_ref, acc_ref)

    def step(start, width, causal):
        ks = pl.ds(pl.multiple_of(start, tile), width)
        for pair in range(MLA_HEADS // 2):
            res = []
            for head in (2 * pair, 2 * pair + 1):
                lo = head * HEAD_TILE
                t = _dot_nt(q_ref[:, lo:lo + HEAD_TILE], k_ref[ks, lo:lo + HEAD_TILE]) * scale
                if causal:
                    t = jnp.where(col <= row, t, -jnp.inf)
                res.append(_head_step(t, v_ref[ks, pair * LANES:(pair + 1) * LANES], m_ref, l_ref, head))
            _pair_update(acc_ref, pair, res[0], res[1], low)

    def wide(j, carry):
        step(j * (2 * tile), 2 * tile, False)
        return carry

    lax.fori_loop(0, i // 2, wide, 0)

    @pl.when(i % 2 == 1)
    def _():
        step((i - 1) * tile, tile, False)

    step(i * tile, tile, True)
    for pair in range(MLA_HEADS // 2):
        o_ref[:, pair * LANES:(pair + 1) * LANES] = _pair_out(l_ref, acc_ref, pair, 2 * pair, 2 * pair + 1, low)


def _mla_prompt(q, k, v):
    b, t, nq = q.shape
    tile = min(ATT_TILE, t)
    nv = v.shape[-1]
    return pl.pallas_call(
        functools.partial(_mla_prompt_body, tile=tile),
        grid=(b, t // tile),
        in_specs=[pl.BlockSpec((None, tile, nq), lambda bi, i: (bi, i, 0)),
                  pl.BlockSpec((None, t, nq), lambda bi, i: (bi, 0, 0)),
                  pl.BlockSpec((None, t, nv), lambda bi, i: (bi, 0, 0))],
        out_specs=pl.BlockSpec((None, tile, nv), lambda bi, i: (bi, i, 0)),
        out_shape=jax.ShapeDtypeStruct((b, t, nv), BF16),
        scratch_shapes=[pltpu.VMEM((MLA_HEADS, tile, LANES), F32), pltpu.VMEM((MLA_HEADS, tile, LANES), F32),
                        pltpu.VMEM((MLA_HEADS // 2, tile, LANES), F32)],
        compiler_params=_params(("arbitrary", "arbitrary")),
        name="mla_prompt",
    )(q, k, v)


INT_MIN = -2 ** 31


def _key_to_f32(key):
    bits = jnp.where(key < 0, key ^ jnp.int32(0x7FFFFFFF), key)
    return lax.bitcast_convert_type(bits, F32)


def _kth_largest(count_ge, k, rows, settled):
    cnt0 = count_ge(jnp.zeros((rows, 1), F32))
    pos = cnt0 >= k
    cur0 = jnp.where(pos, jnp.int32(0), jnp.int32(INT_MIN))
    cnt0 = jnp.where(pos, cnt0, jnp.int32(2 ** 30))

    def cond(state):
        it, _, cnt = state
        open_rows = jnp.where(settled | (cnt == k), 0.0, 1.0)
        return (it < 31) & (jnp.max(open_rows) > 0.5)

    def body(state):
        it, cur, cnt = state
        cand = cur + lax.shift_left(jnp.int32(1), jnp.int32(30) - it)
        c = count_ge(_key_to_f32(cand))
        ok = c >= k
        return it + 1, jnp.where(ok, cand, cur), jnp.where(ok, c, cnt)

    _, cur, cnt = lax.while_loop(cond, body, (jnp.int32(0), cur0, cnt0))
    return _key_to_f32(cur), cnt


def _tie_cut(count_eq_below, need, n_cols, rows):
    bits = max(1, int(math.ceil(math.log2(n_cols))))

    def body(it, cur):
        cand = cur + lax.shift_left(jnp.int32(1), jnp.int32(bits - 1) - it)
        return jnp.where(count_eq_below(cand) < need, cand, cur)

    return lax.fori_loop(0, bits, body, jnp.zeros((rows, 1), jnp.int32))


def _dsa_prompt_body(iq_ref, iw_ref, dqp_ref, ikd_ref, dkb_ref, dvb_ref, bias_ref, far_ref, o_ref,
                     s_ref, iqp_ref, wb_ref, cut_ref, cand_ref, pint_ref, m_ref, l_ref, acc_ref, *, tile, k_sel):
    i = pl.program_id(1)
    lane = lax.broadcasted_iota(jnp.int32, (tile, LANES), 1)
    low = lane < DSA_HEAD_DIM
    row = lax.broadcasted_iota(jnp.int32, (tile, tile), 0)
    col = lax.broadcasted_iota(jnp.int32, (tile, tile), 1)
    reps = tile // LANES

    for h in range(IDX_HEADS):
        t = iq_ref[:, (h // 2) * LANES:(h // 2 + 1) * LANES]
        keep = low if h % 2 == 0 else jnp.logical_not(low)
        iqp_ref[h] = jnp.where(keep, t, jnp.zeros_like(t))
        wb_ref[h] = jnp.broadcast_to(iw_ref[:, h:h + 1], (tile, LANES))

    def score_block(kb, causal):
        ks = pl.ds(pl.multiple_of(kb * tile, tile), tile)
        kt = ikd_ref[ks, :]
        sc = jnp.zeros((tile, tile), F32)
        for h in range(IDX_HEADS):
            w = jnp.concatenate([wb_ref[h]] * reps, axis=1)
            sc = sc + jnp.maximum(_dot_nt(iqp_ref[h], kt), 0.0) * w
        if causal:
            sc = jnp.where(col <= row, sc, -jnp.inf)
        s_ref[kb] = sc

    def score_loop(kb, carry):
        score_block(kb, False)
        return carry

    lax.fori_loop(0, i, score_loop, 0)
    score_block(i, True)
    s_ref[i + 1] = jnp.full((tile, tile), -jnp.inf, F32)

    def lane_fold(x):
        out = x[:, :LANES]
        for r in range(1, reps):
            out = out + x[:, r * LANES:(r + 1) * LANES]
        return out

    n_pairs = i // 2 + 1

    half = tile // 2
    col_h = lax.broadcasted_iota(jnp.int32, (half, tile), 1)

    def masked_count(pred):
        accs = []
        for r0 in range(0, tile, half):
            rows = slice(r0, r0 + half)

            def body(j, acc, rows=rows):
                for kb in (2 * j, 2 * j + 1):
                    acc = acc + lane_fold(jnp.where(pred(s_ref[kb, rows, :], kb, rows), 1.0, 0.0))
                return acc

            accs.append(lax.fori_loop(0, n_pairs, body, jnp.zeros((half, LANES), F32)))
        return jnp.sum(jnp.concatenate(accs, axis=0), axis=-1, keepdims=True).astype(jnp.int32)

    rowval = lambda ref, rows: jnp.concatenate([ref[rows, :]] * reps, axis=1)

    def count_ge(cand):
        cand_ref[...] = jnp.broadcast_to(cand, (tile, LANES))
        return masked_count(lambda s, kb, rows: s >= rowval(cand_ref, rows))

    def count_tied_below(p):
        pint_ref[...] = jnp.broadcast_to(p, (tile, LANES))
        return masked_count(lambda s, kb, rows: (s == rowval(cand_ref, rows))
                            & (kb * tile + col_h < rowval(pint_ref, rows)))

    n_adm = i * tile + lax.broadcasted_iota(jnp.int32, (tile, 1), 0) + 1
    keep_all = n_adm <= k_sel
    thr, n_ge = _kth_largest(count_ge, k_sel, tile, keep_all)
    thr = jnp.where(keep_all, F32_LOWEST, thr)
    tied = jnp.logical_not(keep_all) & (n_ge > k_sel)
    cut_ref[...] = jnp.full(cut_ref.shape, 2 ** 30, jnp.int32)

    @pl.when(jnp.max(jnp.where(tied, 1.0, 0.0)) > 0.5)
    def _():
        cand_ref[...] = jnp.broadcast_to(thr, (tile, LANES))
        need = k_sel - masked_count(lambda s, kb, rows: s > rowval(cand_ref, rows))
        cut_ref[...] = _tie_cut(count_tied_below, need, (s_ref.shape[0] - 1) * tile, tile)

    cut = cut_ref[...]

    def mask_loop(j, carry):
        for kb in (2 * j, 2 * j + 1):
            s = s_ref[kb]
            sel = (s > thr) | ((s == thr) & (kb * tile + col <= cut))
            s_ref[kb] = jnp.where(sel, 0.0, NEG)
        return carry

    lax.fori_loop(0, n_pairs, mask_loop, 0)

    _softmax_init(m_ref, l_ref, acc_ref)

    def step(kb, nblk, kind):
        ks = pl.ds(pl.multiple_of(kb * tile, tile), nblk * tile)
        kt = dkb_ref[ks, :]
        vt = dvb_ref[ks, :]
        mb = jnp.concatenate([s_ref[kb + n] for n in range(nblk)], axis=1)
        for pair in range(DSA_GROUP):
            res = []
            for head in (pair, pair + DSA_GROUP):
                s = _dot_nt(dqp_ref[:, head * LANES:(head + 1) * LANES], kt) * LOG2E + mb
                if kind == "far":
                    res.append(_head_step(s, vt, m_ref, l_ref, head, shift=far_ref[head:head + 1, :]))
                else:
                    res.append(_head_step(s + bias_ref[head, 0 if kind == "diag" else 1], vt, m_ref, l_ref, head))
            _pair_update(acc_ref, pair, res[0], res[1], low)

    n_far = jnp.maximum(i - 1, 0)

    def far_wide(j, carry):
        step(2 * j, 2, "far")
        return carry

    lax.fori_loop(0, n_far // 2, far_wide, 0)

    @pl.when(n_far % 2 == 1)
    def _():
        step(n_far - 1, 1, "far")

    @pl.when(i >= 1)
    def _():
        step(i - 1, 1, "prev")

    step(i, 1, "diag")
    for pair in range(DSA_GROUP):
        o_ref[:, pair * LANES:(pair + 1) * LANES] = _pair_out(l_ref, acc_ref, pair, pair, pair + DSA_GROUP, low)


def _dsa_prompt(pr, bias_p, bias_far, k_sel):
    iq, iw, dqp, ikd, dkb, dvb = pr["iq"], pr["iw"], pr["dqp"], pr["ikd"], pr["dkb"], pr["dvb"]
    b, t, _ = iq.shape
    tile = min(ATT_TILE, t)
    assert bias_p.shape[-1] == tile and tile >= MAX_DISTANCE
    blk = lambda w: pl.BlockSpec((None, tile, w), lambda bi, i: (bi, i, 0))
    seq = lambda w: pl.BlockSpec((None, t, w), lambda bi, i: (bi, 0, 0))
    return pl.pallas_call(
        functools.partial(_dsa_prompt_body, tile=tile, k_sel=k_sel),
        grid=(b, t // tile),
        in_specs=[blk(iq.shape[-1]), blk(LANES), blk(dqp.shape[-1]), seq(LANES), seq(LANES), seq(LANES),
                  _resident(bias_p.shape), _resident(bias_far.shape)],
        out_specs=blk(DSA_HEADS * DSA_HEAD_DIM),
        out_shape=jax.ShapeDtypeStruct((b, t, DSA_HEADS * DSA_HEAD_DIM), BF16),
        scratch_shapes=[pltpu.VMEM((t // tile + 1, tile, tile), F32),
                        pltpu.VMEM((IDX_HEADS, tile, LANES), BF16),
                        pltpu.VMEM((IDX_HEADS, tile, LANES), F32),
                        pltpu.VMEM((tile, 1), jnp.int32),
                        pltpu.VMEM((tile, LANES), F32),
                        pltpu.VMEM((tile, LANES), jnp.int32),
                        pltpu.VMEM((DSA_HEADS, tile, LANES), F32), pltpu.VMEM((DSA_HEADS, tile, LANES), F32),
                        pltpu.VMEM((DSA_GROUP, tile, LANES), F32)],
        compiler_params=_params(("arbitrary", "arbitrary")),
        name="dsa_prompt",
    )(iq, iw, dqp, ikd, dkb, dvb, bias_p, bias_far)


def _sample_q_body(q_ref, wukt_ref, ql_ref, qr_ref):
    for h in range(MLA_HEADS):
        qt = q_ref[:, h * HEAD_TILE:(h + 1) * HEAD_TILE]
        ql_ref[:, h * KV_LORA:(h + 1) * KV_LORA] = _dot(qt, wukt_ref[h]).astype(BF16)
        qr = pltpu.roll(qt.astype(F32), LANES - ROPE_LANE0, axis=1)[:, :MLA_ROPE]
        qr_ref[:, h * MLA_ROPE:(h + 1) * MLA_ROPE] = qr.astype(BF16)


def _sample_q(q, w_ukt):
    n = q.shape[0]
    return pl.pallas_call(
        _sample_q_body,
        out_shape=[jax.ShapeDtypeStruct((n, MLA_HEADS * KV_LORA), BF16),
                   jax.ShapeDtypeStruct((n, MLA_HEADS * MLA_ROPE), BF16)],
        compiler_params=pltpu.CompilerParams(vmem_limit_bytes=VMEM_LIMIT_BYTES),
        name="sample_q",
    )(q, w_ukt)


PAGE_SLOTS = 3


def _page_pipeline(pt_ref, caches, bufs, sems, layer, g_pages):
    b, c = pl.program_id(0), pl.program_id(1)
    nb, nc = pl.num_programs(0), pl.num_programs(1)
    n_steps = nb * nc
    ahead = PAGE_SLOTS - 1
    step = b * nc + c
    slot = step % PAGE_SLOTS

    def page_copy(k, page, sl, g):
        return pltpu.make_async_copy(caches[k].at[layer, page], bufs[k].at[sl, g], sems.at[sl, k])

    def starter(s):
        sw = jnp.where(s >= n_steps, s - n_steps, s)
        row, col0, sl = sw // nc, (sw % nc) * g_pages, s % PAGE_SLOTS

        def start(g):
            page = pt_ref[row, col0 + g]
            for k in range(len(caches)):
                page_copy(k, page, sl, g).start()
        return start

    def wait_all(sl):
        for g in range(g_pages):
            for k in range(len(caches)):
                page_copy(k, 0, sl, g).wait()

    @pl.when(step == 0)
    def _():
        for d in range(ahead):
            start = starter(jnp.int32(d))
            for g in range(g_pages):
                start(g)

    wait_all(slot)
    prefetch = starter(step + ahead)

    def drain():
        @pl.when(step == n_steps - 1)
        def _():
            for d in range(1, ahead + 1):
                wait_all((step + d) % PAGE_SLOTS)

    return slot, prefetch, drain


def _decode1_body(pt_ref, ckv_hbm, kr_hbm, ik_hbm, ql_ref, qr_ref, cn_ref, kn_ref, iq_ref, iw_ref,
                  o_ref, sc_ref, ckv_buf, kr_buf, ik_buf, sems, m_ref, l_ref, acc_ref,
                  *, layer, g_pages, scale):
    slot, prefetch, drain = _page_pipeline(pt_ref, (ckv_hbm, kr_hbm, ik_hbm), (ckv_buf, kr_buf, ik_buf),
                                           sems, layer, g_pages)
    ckv = [ckv_buf.at[slot, g] for g in range(g_pages)]
    krp = [kr_buf.at[slot, g] for g in range(g_pages)]
    ikp = [ik_buf.at[slot, g] for g in range(g_pages)]
    c = pl.program_id(1)
    nc = pl.num_programs(1)

    @pl.when(c == 0)
    def _():
        m_ref[...] = jnp.full(m_ref.shape, -jnp.inf, F32)
        l_ref[...] = jnp.zeros(l_ref.shape, F32)
        acc_ref[...] = jnp.zeros(acc_ref.shape, F32)

    ql = ql_ref[...].astype(F32)
    qr = qr_ref[...].astype(F32)
    iq = iq_ref[...].astype(F32)
    iw = iw_ref[...]
    logits, scores = [], []
    for g in range(g_pages):
        prefetch(g)
        logits.append(_dot_nt(ql, ckv[g][...]) + _dot(qr, krp[g][...]))
        s = jnp.maximum(_dot(iq, ikp[g][...]), 0.0) * iw
        scores.append(jnp.sum(s, axis=0, keepdims=True))
    sc_ref[...] = jnp.concatenate(scores, axis=1)
    t = jnp.concatenate(logits, axis=1) * scale
    m_old = m_ref[...]
    m_new = jnp.maximum(m_old, jnp.max(t, axis=-1, keepdims=True))
    p = jnp.exp2(t - m_new)
    alpha = jnp.exp2(m_old - m_new)
    l_ref[...] = alpha * l_ref[...] + jnp.sum(p, axis=-1, keepdims=True)
    m_ref[...] = m_new
    pv = jnp.zeros(acc_ref.shape, F32)
    for g in range(g_pages):
        pv = pv + _dot(p[:, g * LANES:(g + 1) * LANES], ckv[g][...])
    acc_ref[...] = acc_ref[...] * alpha + pv

    @pl.when(c == nc - 1)
    def _():
        cn = cn_ref[...].astype(BF16).astype(F32)
        kn = kn_ref[...].astype(BF16).astype(F32)
        ln = (jnp.sum(ql * cn, axis=-1, keepdims=True) + jnp.sum(qr * kn, axis=-1, keepdims=True)) * scale
        m_old = m_ref[...]
        m_new = jnp.maximum(m_old, ln)
        pn = jnp.exp2(ln - m_new)
        alpha = jnp.exp2(m_old - m_new)
        l = alpha * l_ref[...] + pn
        acc = acc_ref[...] * alpha + pn.astype(BF16).astype(F32) * cn
        o_ref[...] = (acc / l).astype(BF16)

    drain()


def _decode1(layer, page_table, cache_ckv, cache_krope, cache_idx_k, ql, qr, ckv_new, kr_new, iq, iw):
    bd, n_pages = page_table.shape
    g_pages = min(PAGES_PER_STEP, n_pages)
    nc = n_pages // g_pages
    gw = g_pages * LANES

    per_seq = lambda r, w: pl.BlockSpec((None, r, w), lambda b, c, pt: (b, 0, 0))
    hbm = pl.BlockSpec(memory_space=pl.ANY)
    grid_spec = pltpu.PrefetchScalarGridSpec(
        num_scalar_prefetch=1,
        grid=(bd, nc),
        in_specs=[hbm, hbm, hbm,
                  per_seq(MLA_HEADS, KV_LORA), per_seq(MLA_HEADS, MLA_ROPE), per_seq(1, KV_LORA),
                  per_seq(1, MLA_ROPE), per_seq(IDX_HEADS, IDX_DIM), per_seq(IDX_HEADS, 1)],
        out_specs=[per_seq(MLA_HEADS, KV_LORA),
                   pl.BlockSpec((None, None, 1, gw), lambda b, c, pt: (c, b, 0, 0))],
        scratch_shapes=[pltpu.VMEM((PAGE_SLOTS, g_pages, LANES, KV_LORA), F32),
                        pltpu.VMEM((PAGE_SLOTS, g_pages, MLA_ROPE, LANES), F32),
                        pltpu.VMEM((PAGE_SLOTS, g_pages, IDX_DIM, LANES), F32),
                        pltpu.SemaphoreType.DMA((PAGE_SLOTS, 3)),
                        pltpu.VMEM((MLA_HEADS, 1), F32), pltpu.VMEM((MLA_HEADS, 1), F32),
                        pltpu.VMEM((MLA_HEADS, KV_LORA), F32)],
    )
    o_lat, scores = pl.pallas_call(
        functools.partial(_decode1_body, layer=layer, g_pages=g_pages, scale=MLA_SCALE * LOG2E),
        grid_spec=grid_spec,
        out_shape=[jax.ShapeDtypeStruct((bd, MLA_HEADS, KV_LORA), BF16),
                   jax.ShapeDtypeStruct((nc, bd, 1, gw), F32)],
        compiler_params=_params(("arbitrary", "arbitrary")),
        name="decode_mla_idx",
    )(page_table, cache_ckv, cache_krope, cache_idx_k, ql, qr, ckv_new, kr_new, iq, iw)
    return o_lat, scores.reshape(nc, bd, gw)


def _sample_topk_body(s_ref, iq_ref, ikd_ref, iw_ref, mb_ref, mbn_ref, *, k_sel):
    nc, rows, gw = s_ref.shape
    col = lax.broadcasted_iota(jnp.int32, (rows, gw), 1)
    reps = gw // LANES

    prod = iq_ref[...].astype(F32) * jnp.concatenate([ikd_ref[...].astype(F32)] * (IDX_HEADS // 2), axis=1)
    s_new = jnp.zeros((rows, 1), F32)
    for h in range(IDX_HEADS):
        dot_h = jnp.sum(prod[:, h * IDX_DIM:(h + 1) * IDX_DIM], axis=-1, keepdims=True)
        s_new = s_new + jnp.maximum(dot_h, 0.0) * iw_ref[:, h:h + 1]

    def lane_fold(x):
        out = x[:, :LANES]
        for r in range(1, reps):
            out = out + x[:, r * LANES:(r + 1) * LANES]
        return out

    def masked_count(pred, pred_new):
        acc = jnp.zeros((rows, LANES), F32)
        for c in range(nc):
            acc = acc + lane_fold(jnp.where(pred(s_ref[c], c), 1.0, 0.0))
        cnt = jnp.sum(acc, axis=-1, keepdims=True) + jnp.where(pred_new, 1.0, 0.0)
        return cnt.astype(jnp.int32)

    n_cols = nc * gw
    thr, _ = _kth_largest(lambda cand: masked_count(lambda s, c: s >= cand, s_new >= cand), k_sel, rows,
                          jnp.zeros((rows, 1), jnp.bool_))
    need = k_sel - masked_count(lambda s, c: s > thr, s_new > thr)
    cut = _tie_cut(lambda p: masked_count(lambda s, c: (s == thr) & (c * gw + col < p),
                                          (s_new == thr) & (n_cols < p)),
                   need, n_cols + 1, rows)
    for c in range(nc):
        s = s_ref[c]
        sel = (s > thr) | ((s == thr) & (c * gw + col <= cut))
        mb_ref[c] = jnp.where(sel, 0.0, NEG)
    sel_new = (s_new > thr) | ((s_new == thr) & (n_cols <= cut))
    mbn_ref[...] = jnp.broadcast_to(jnp.where(sel_new, 0.0, NEG), mbn_ref.shape)


def _sample_topk(scores, iq, ikd, iw, k_sel):
    nc, bd, gw = scores.shape
    rows = min(32, bd)
    return pl.pallas_call(
        functools.partial(_sample_topk_body, k_sel=k_sel),
        grid=(bd // rows,),
        in_specs=[pl.BlockSpec((nc, rows, gw), lambda r: (0, r, 0)),
                  pl.BlockSpec((rows, iq.shape[-1]), lambda r: (r, 0)),
                  pl.BlockSpec((rows, LANES), lambda r: (r, 0)),
                  pl.BlockSpec((rows, LANES), lambda r: (r, 0))],
        out_specs=[pl.BlockSpec((nc, rows, gw), lambda r: (0, r, 0)),
                   pl.BlockSpec((rows, LANES), lambda r: (r, 0))],
        out_shape=[jax.ShapeDtypeStruct((nc, bd, gw), F32), jax.ShapeDtypeStruct((bd, LANES), F32)],
        compiler_params=_params(("arbitrary",)),
        name="sample_topk",
    )(scores, iq, ikd, iw)


def _decode2_body(pt_ref, k_hbm, v_hbm, q_ref, mb_ref, mbn_ref, bias_ref, dkn_ref, dvn_ref,
                  o_ref, k_buf, v_buf, sems, m_ref, l_ref, acc_ref, *, layer, g_pages):
    slot, prefetch, drain = _page_pipeline(pt_ref, (k_hbm, v_hbm), (k_buf, v_buf), sems, layer, g_pages)
    kp = [k_buf.at[slot, g] for g in range(g_pages)]
    vp = [v_buf.at[slot, g] for g in range(g_pages)]
    c = pl.program_id(1)
    nc = pl.num_programs(1)

    @pl.when(c == 0)
    def _():
        m_ref[...] = jnp.full(m_ref.shape, -jnp.inf, F32)
        l_ref[...] = jnp.zeros(l_ref.shape, F32)
        acc_ref[...] = jnp.zeros(acc_ref.shape, F32)

    q = q_ref[...].astype(F32)
    mb = mb_ref[...]
    far = bias_ref[1]
    last = jnp.where(c == nc - 1, bias_ref[0], far)
    logits = []
    for g in range(g_pages):
        prefetch(g)
        bias = last if g == g_pages - 1 else far
        logits.append(_dot(q, kp[g][...]) * LOG2E + (bias + mb[:, g * LANES:(g + 1) * LANES]))
    t = jnp.concatenate(logits, axis=1)
    m_old = m_ref[...]
    m_new = jnp.maximum(m_old, jnp.max(t, axis=-1, keepdims=True))
    p = jnp.exp2(t - m_new)
    alpha = jnp.exp2(m_old - m_new)
    l_ref[...] = alpha * l_ref[...] + jnp.sum(p, axis=-1, keepdims=True)
    m_ref[...] = m_new
    pv = jnp.zeros(acc_ref.shape, F32)
    for g in range(g_pages):
        pv = pv + _dot_nt(p[:, g * LANES:(g + 1) * LANES], vp[g][...])
    acc_ref[...] = acc_ref[...] * alpha + pv

    @pl.when(c == nc - 1)
    def _():
        dkn = dkn_ref[...].astype(BF16).astype(F32)
        dvn = dvn_ref[...].astype(BF16).astype(F32)
        ln = (jnp.sum(q * dkn, axis=-1, keepdims=True) * LOG2E
              + (bias_ref[2][:, 0:1] + mbn_ref[:, 0:1]))
        m_old = m_ref[...]
        m_new = jnp.maximum(m_old, ln)
        pn = jnp.exp2(ln - m_new)
        alpha = jnp.exp2(m_old - m_new)
        l = alpha * l_ref[...] + pn
        out = (acc_ref[...] * alpha + pn.astype(BF16).astype(F32) * dvn) / l
        rowi = lax.broadcasted_iota(jnp.int32, out.shape, 0)
        out = jnp.where(rowi < DSA_GROUP, out, pltpu.roll(out, DSA_HEAD_DIM, axis=1))
        o_ref[...] = out[:, :DSA_HEAD_DIM].astype(BF16)

    drain()


def _decode2(layer, page_table, cache_k, cache_v, dqp, mb, mbn, bias_s, dk_new, dv_new):
    bd, n_pages = page_table.shape
    g_pages = min(PAGES_PER_STEP, n_pages)
    nc = n_pages // g_pages
    gw = g_pages * LANES

    per_seq = lambda r, w: pl.BlockSpec((None, r, w), lambda b, c, pt: (b, 0, 0))
    hbm = pl.BlockSpec(memory_space=pl.ANY)
    grid_spec = pltpu.PrefetchScalarGridSpec(
        num_scalar_prefetch=1,
        grid=(bd, nc),
        in_specs=[hbm, hbm,
                  per_seq(DSA_HEADS, LANES),
                  pl.BlockSpec((None, None, 1, gw), lambda b, c, pt: (c, b, 0, 0)),
                  per_seq(1, LANES),
                  pl.BlockSpec(bias_s.shape, lambda b, c, pt: (0, 0, 0)),
                  per_seq(1, LANES), per_seq(1, LANES)],
        out_specs=per_seq(DSA_HEADS, DSA_HEAD_DIM),
        scratch_shapes=[pltpu.VMEM((PAGE_SLOTS, g_pages, LANES, LANES), F32),
                        pltpu.VMEM((PAGE_SLOTS, g_pages, LANES, LANES), F32),
                        pltpu.SemaphoreType.DMA((PAGE_SLOTS, 2)),
                        pltpu.VMEM((DSA_HEADS, 1), F32), pltpu.VMEM((DSA_HEADS, 1), F32),
                        pltpu.VMEM((DSA_HEADS, LANES), F32)],
    )
    return pl.pallas_call(
        functools.partial(_decode2_body, layer=layer, g_pages=g_pages),
        grid_spec=grid_spec,
        out_shape=jax.ShapeDtypeStruct((bd, DSA_HEADS, DSA_HEAD_DIM), BF16),
        compiler_params=_params(("arbitrary", "arbitrary")),
        name="decode_dsa",
    )(page_table, cache_k, cache_v,
      dqp, mb.reshape(nc, bd, 1, gw), mbn.reshape(bd, 1, LANES), bias_s, dk_new, dv_new)


def _sample_out_body(ol_ref, w_ref, o_ref):
    o_ref[...] = _dot(ol_ref[...], w_ref[...]).astype(BF16)


def _sample_out(o_lat, w_uv_bd):
    n = o_lat.shape[0]
    return pl.pallas_call(
        _sample_out_body,
        out_shape=jax.ShapeDtypeStruct((n, w_uv_bd.shape[1]), BF16),
        compiler_params=pltpu.CompilerParams(vmem_limit_bytes=VMEM_LIMIT_BYTES),
        name="sample_out",
    )(o_lat, w_uv_bd)


DSA_HEAD_ORDER = tuple(h + g * DSA_GROUP for h in range(DSA_GROUP) for g in range(DSA_KV_HEADS))


def _layer_weights(l, ffn_w_gate, ffn_w_up, ffn_w_down, w_in, mla_q_norm, mla_w_uq, mla_kv_norm,
                   mla_w_uk, mla_w_uv, w_out):
    d = w_in.shape[1]
    z = lambda n: jnp.zeros((d, n), F32)
    cq, ckv, kr, dq, dk, dv, iq, ik, iw = jnp.split(w_in[l], np.cumsum(IN_SPLITS)[:-1], axis=1)
    half = MLA_ROPE // 2
    kr_rot = jnp.concatenate([-kr[:, half:], kr[:, :half]], axis=1)
    pad_rope = lambda a: jnp.concatenate([z(ROPE_LANE0), a, z(LANES - ROPE_LANE0 - MLA_ROPE)], axis=1)
    w_in_p = jnp.concatenate([cq, ckv, pad_rope(kr), pad_rope(kr_rot), dq, dk, dv, iq, ik, ik,
                              iw, z(LANES - IDX_HEADS)], axis=1).astype(BF16)
    assert w_in_p.shape[1] == P_COLS

    uq = mla_w_uq[l].reshape(Q_LORA, MLA_HEADS, MLA_NOPE + MLA_ROPE)
    nope, r1, r2 = uq[..., :MLA_NOPE], uq[..., MLA_NOPE:MLA_NOPE + half], uq[..., MLA_NOPE + half:]
    zq = lambda n: jnp.zeros((Q_LORA, MLA_HEADS, n), F32)
    tail = HEAD_TILE - MLA_NOPE - MLA_ROPE
    w_a = jnp.concatenate([nope, r1, r2, zq(tail)], axis=-1).reshape(Q_LORA, -1)
    w_b = jnp.concatenate([zq(MLA_NOPE), -r2, r1, zq(tail)], axis=-1).reshape(Q_LORA, -1)
    w_ab = jnp.concatenate([w_a, w_b], axis=1).astype(BF16)

    uk = mla_w_uk[l]
    w_ukp = jnp.concatenate([uk, jnp.zeros((KV_LORA, MLA_HEADS, HEAD_TILE - MLA_NOPE), F32)],
                            axis=-1).reshape(KV_LORA, -1).astype(BF16)
    w_ukt = jnp.concatenate([jnp.transpose(uk, (1, 2, 0)),
                             jnp.zeros((MLA_HEADS, HEAD_TILE - MLA_NOPE, KV_LORA), F32)], axis=1).astype(BF16)
    uv = mla_w_uv[l]
    w_uv = uv.reshape(KV_LORA, -1).astype(BF16)
    eye = jnp.eye(MLA_HEADS, dtype=F32)
    w_uv_bd = (jnp.transpose(uv, (1, 0, 2))[:, :, None, :] * eye[:, None, :, None]).reshape(
        MLA_HEADS * KV_LORA, MLA_HEADS * MLA_V).astype(BF16)

    mla_w = MLA_HEADS * MLA_V
    wo = w_out[l]
    wo_dsa = wo[mla_w:].reshape(DSA_HEADS, DSA_HEAD_DIM, -1)[jnp.asarray(DSA_HEAD_ORDER)]
    wo_p = jnp.concatenate([wo[:mla_w], wo_dsa.reshape(DSA_HEADS * DSA_HEAD_DIM, -1)], axis=0).astype(BF16)

    nch = D_FF // FF_CHUNK
    chunk_cols = lambda w: jnp.transpose(w.reshape(d, nch, FF_CHUNK), (1, 0, 2)).astype(BF16)
    ffn = [(chunk_cols(ffn_w_gate[l, j]), chunk_cols(ffn_w_up[l, j]),
            ffn_w_down[l, j].reshape(nch, FF_CHUNK, d).astype(BF16)) for j in range(2)]
    return dict(w_in=w_in_p, q_norm=mla_q_norm[l].reshape(1, -1), w_ab=w_ab,
                kv_norm=mla_kv_norm[l].reshape(1, -1), w_ukp=w_ukp, w_ukt=w_ukt, w_uv=w_uv,
                w_uv_bd=w_uv_bd, w_out=wo_p, ffn=ffn)


def _rope_tables(pos):
    half = MLA_ROPE // 2
    inv = ROPE_THETA ** (-jnp.arange(half, dtype=F32) / half)
    ang = pos.astype(F32)[:, None] * inv[None, :]
    n = pos.shape[0]
    tail = jnp.zeros((n, LANES - ROPE_LANE0 - MLA_ROPE), F32)
    cos_t = jnp.concatenate([jnp.ones((n, ROPE_LANE0), F32), jnp.cos(ang), jnp.cos(ang), tail], axis=1)
    sin_t = jnp.concatenate([jnp.zeros((n, ROPE_LANE0), F32), jnp.sin(ang), jnp.sin(ang), tail], axis=1)
    return cos_t, sin_t


def kernel(x_prompt, x_sample, cache_ckv, cache_krope, cache_k, cache_v, cache_idx_k, page_table,
           c_prompt, c_sample, norm_g, w_ada, b_ada, ffn_w_gate, ffn_w_up, ffn_w_down, w_in,
           mla_q_norm, mla_w_uq, mla_kv_norm, mla_w_uk, mla_w_uv, w_out, rel_bias, final_norm):
    b, t, d = x_prompt.shape
    bd, dec_seq, _ = x_sample.shape
    depth = w_in.shape[0]
    n_pages = page_table.shape[1]
    page = cache_ckv.shape[2]
    past = n_pages * page
    assert dec_seq == 1 and page == LANES and d == D_MODEL
    assert t % min(ATT_TILE, t) == 0 and n_pages % min(PAGES_PER_STEP, n_pages) == 0
    k_prompt = min(TOPK_MAX, t // 4)
    k_sample = min(TOPK_MAX, (past + dec_seq) // 4)

    mod = _adaln(jnp.concatenate([c_prompt, c_sample], axis=0), w_ada, b_ada)
    cos_p, sin_p = _rope_tables(jnp.arange(t, dtype=jnp.int32))
    cos_s, sin_s = _rope_tables(jnp.full((bd,), past, dtype=jnp.int32))
    tile = min(ATT_TILE, t)
    bias_p, bias_s = _bias_tables(rel_bias, tile, tile)
    bias_s = jnp.transpose(bias_s, (1, 0, 2))
    slot_minor = lambda a: jnp.swapaxes(a, 2, 3)
    kv_slot_minor = lambda a: jnp.transpose(a, (0, 1, 3, 4, 2)).reshape(a.shape[:2] + (LANES, page))
    krope_t, idx_k_t = slot_minor(cache_krope), slot_minor(cache_idx_k)
    ck, cv = kv_slot_minor(cache_k), kv_slot_minor(cache_v)
    head_order = jnp.asarray(DSA_HEAD_ORDER)

    hp = x_prompt
    hs = x_sample.reshape(1, bd, d)
    rows_p, rows_s = [], []
    for l in range(depth):
        lw = _layer_weights(l, ffn_w_gate, ffn_w_up, ffn_w_down, w_in, mla_q_norm, mla_w_uq,
                            mla_kv_norm, mla_w_uk, mla_w_uv, w_out)
        mod_p = mod[l, :b].reshape(b, 1, -1)
        mod_s = mod[l, b:].reshape(1, bd, -1)
        last = l == depth - 1

        hp = _ffn(hp, mod_p, 0, norm_g[l, 0], *lw["ffn"][0])
        pr = _proj(hp, mod_p, norm_g[l, 1], lw, cos_p, sin_p)
        a_p = _mla_prompt(pr["q"], pr["k"], pr["v"])
        b_p = _dsa_prompt(pr, bias_p, bias_s[1], k_prompt)
        hp = _ffn(hp, mod_p, 6, norm_g[l, 2], *lw["ffn"][1], mix=(a_p, b_p, lw["w_out"], 5),
                  final_g=final_norm if last else None)
        rows_p.append((pr["ckv"], pr["kr"], pr["dk"], pr["dv"], pr["ik"]))

        hs = _ffn(hs, mod_s, 0, norm_g[l, 0], *lw["ffn"][0])
        sr = {name: val[0] for name, val in _proj(hs, mod_s, norm_g[l, 1], lw, cos_s, sin_s).items()}
        ql, qr = _sample_q(sr["q"], lw["w_ukt"])
        o_lat, scores = _decode1(
            l, page_table, cache_ckv, krope_t, idx_k_t,
            ql.reshape(bd, MLA_HEADS, KV_LORA), qr.reshape(bd, MLA_HEADS, MLA_ROPE),
            sr["ckv"].reshape(bd, 1, KV_LORA), sr["kr"].reshape(bd, 1, MLA_ROPE),
            sr["iq"].reshape(bd, IDX_HEADS, IDX_DIM), sr["iw"][:, :IDX_HEADS].reshape(bd, IDX_HEADS, 1))
        mb, mbn = _sample_topk(scores, sr["iq"], sr["ikd"], sr["iw"], k_sample)
        o_dsa = _decode2(l, page_table, ck, cv, sr["dqp"].reshape(bd, DSA_HEADS, LANES), mb, mbn, bias_s,
                         sr["dk"].reshape(bd, 1, LANES), sr["dv"].reshape(bd, 1, LANES))
        a_s = _sample_out(o_lat.reshape(bd, MLA_HEADS * KV_LORA), lw["w_uv_bd"])
        b_s = o_dsa[:, head_order].reshape(bd, DSA_HEADS * DSA_HEAD_DIM)
        hs = _ffn(hs, mod_s, 6, norm_g[l, 2], *lw["ffn"][1],
                  mix=(a_s.reshape(1, bd, -1), b_s.reshape(1, bd, -1), lw["w_out"], 5),
                  final_g=final_norm if last else None)
        rows_s.append((sr["ckv"], sr["kr"], sr["dk"], sr["dv"], sr["ik"]))

    p_ckv, p_kr, p_k, p_v, p_ik = [jnp.stack(x, axis=0) for x in zip(*rows_p)]
    s_ckv, s_kr, s_k, s_v, s_ik = [jnp.stack(x, axis=0) for x in zip(*rows_s)]
    kv5 = lambda a, n: a.reshape(depth, n[0], n[1], DSA_KV_HEADS, DSA_HEAD_DIM)
    seq4 = lambda a: a.reshape(depth, bd, dec_seq, a.shape[-1])
    return (hp, hs.reshape(bd, dec_seq, d), p_ckv, p_kr, kv5(p_k, (b, t)), kv5(p_v, (b, t)), p_ik,
            seq4(s_ckv), seq4(s_kr), kv5(s_k, (bd, dec_seq)), kv5(s_v, (bd, dec_seq)), seq4(s_ik))
```
